```python
import math
import jax, jax.numpy as jnp
from jax import lax
import numpy as np

D_MODEL = 1024
BATCH = 4
SEQ = 4096
DEPTH = 4

HEAD_DIM = 64
MIX_WIDTH = D_MODEL
SB_WIDTH = MIX_WIDTH // 2
SB_HEADS = SB_WIDTH // HEAD_DIM
SSM_WIDTH = MIX_WIDTH - SB_WIDTH
SSM_GROUP = 16
SSM_GROUPS = SSM_WIDTH // SSM_GROUP
SSM_STATE = 64
DSA_HEADS = MIX_WIDTH // HEAD_DIM
DSA_KV_HEADS = 4
DSA_REP = DSA_HEADS // DSA_KV_HEADS
DSA_WIDTH = DSA_HEADS * HEAD_DIM
IDX_HEADS = 8
IDX_DIM = 32
TOPK_MAX = 256
D_FF = 4 * D_MODEL
Q_BLOCK = 128
N_EVEN = (DEPTH + 1) // 2
N_ODD = DEPTH // 2
ALPHA = (2 * DEPTH) ** 0.25
BETA = (8 * DEPTH) ** -0.25
LN_EPS = 1e-5
SB_IN = 3 * SB_WIDTH + SSM_WIDTH
DSA_SIZES = (DSA_WIDTH, DSA_KV_HEADS * HEAD_DIM, DSA_KV_HEADS * HEAD_DIM, IDX_HEADS * IDX_DIM, IDX_DIM, IDX_HEADS)
DSA_IN = sum(DSA_SIZES)
DSA_SPLITS = tuple(int(v) for v in np.cumsum(DSA_SIZES)[:-1])

kernel_name = "hybrid_stickbreak_s5_dsa_deepnorm"


def layer_norm(x, g, b):
    xf = x.astype(jnp.float32)
    mu = jnp.mean(xf, axis=-1, keepdims=True)
    var = jnp.mean(jnp.square(xf - mu), axis=-1, keepdims=True)
    y = (xf - mu) * lax.rsqrt(var + LN_EPS) * g.astype(jnp.float32) + b.astype(jnp.float32)
    return y.astype(x.dtype)


def to_blocks(a, nb):
    return jnp.moveaxis(a.reshape((a.shape[0], nb, Q_BLOCK) + a.shape[2:]), 1, 0)


def stick_breaking_attention(q, k, v):
    Bn, S, H, dh = q.shape
    nb = S // Q_BLOCK
    spos = jnp.arange(S)
    scale = dh ** -0.5

    def block(args):
        q_b, bid = args
        tpos = bid * Q_BLOCK + jnp.arange(Q_BLOCK)
        z = jnp.einsum('bqhd,bshd->bhqs', q_b, k).astype(jnp.float32) * scale
        causal = (spos[None, :] < tpos[:, None])[None, None]
        log_beta = jax.nn.log_sigmoid(z)
        log_1mb = jnp.where(causal, jax.nn.log_sigmoid(-z), 0.0)
        csum = jnp.cumsum(log_1mb, axis=-1)
        log_w = log_beta + csum[..., -1:] - csum
        w = jnp.where(causal, jnp.exp(log_w), 0.0).astype(v.dtype)
        return jnp.einsum('bhqs,bshd->bqhd', w, v)

    out = lax.map(block, (to_blocks(q, nb), jnp.arange(nb)))
    return jnp.moveaxis(out, 0, 1).reshape(Bn, S, H * dh)


def _complex_scan_op(e_i, e_j):
    air, aii, bir, bii = e_i
    ajr, aji, bjr, bji = e_j
    ar = ajr * air - aji * aii
    ai = ajr * aii + aji * air
    br = ajr * bir - aji * bii + bjr
    bi = ajr * bii + aji * bir + bji
    return (ar, ai, br, bi)


def s5_ssm(u, log_dt, lam_re, lam_im, b_re, b_im, c_re, c_im, d):
    f32 = jnp.float32
    S = u.shape[1]
    dt = jnp.exp(log_dt.astype(f32))[:, None]
    lr = lam_re.astype(f32)
    li = lam_im.astype(f32)
    mag = jnp.exp(lr * dt)
    abar_r = mag * jnp.cos(li * dt)
    abar_i = mag * jnp.sin(li * dt)
    den = lr * lr + li * li
    nr = abar_r - 1.0
    coef_r = ((nr * lr + abar_i * li) / den)[..., None]
    coef_i = ((abar_i * lr - nr * li) / den)[..., None]
    br = b_re.astype(f32)
    bi = b_im.astype(f32)
    bbar_r = coef_r * br - coef_i * bi
    bbar_i = coef_r * bi + coef_i * br
    uf = u.astype(f32)
    bu_r = jnp.einsum('bsgc,gnc->bsgn', uf, bbar_r)
    bu_i = jnp.einsum('bsgc,gnc->bsgn', uf, bbar_i)
    a_r = jnp.broadcast_to(abar_r, (1, S) + abar_r.shape)
    a_i = jnp.broadcast_to(abar_i, (1, S) + abar_i.shape)
    _, _, x_r, x_i = lax.associative_scan(_complex_scan_op, (a_r, a_i, bu_r, bu_i), axis=1)
    y = (jnp.einsum('gcn,bsgn->bsgc', c_re.astype(f32), x_r)
         - jnp.einsum('gcn,bsgn->bsgc', c_im.astype(f32), x_i)
         + d.astype(f32) * uf)
    return y.astype(u.dtype)


def mixer_sb_ssm(x, w_in, log_dt, lam_re, lam_im, b_re, b_im, c_re, c_im, d, w_glu, b_glu, w_out):
    Bn, S, _ = x.shape
    proj = x @ w_in
    q, k, v, u = jnp.split(proj, [SB_WIDTH, 2 * SB_WIDTH, 3 * SB_WIDTH], axis=-1)
    hs = (Bn, S, SB_HEADS, HEAD_DIM)
    a_out = stick_breaking_attention(q.reshape(hs), k.reshape(hs), v.reshape(hs))
    y = s5_ssm(u.reshape(Bn, S, SSM_GROUPS, SSM_GROUP), log_dt, lam_re, lam_im,
               b_re, b_im, c_re, c_im, d).reshape(Bn, S, SSM_WIDTH)
    y = jax.nn.gelu(y)
    y = y * jax.nn.sigmoid(y @ w_glu + b_glu)
    return jnp.concatenate([a_out, y], axis=-1) @ w_out


def alibi_slopes(n_heads):
    return jnp.asarray(2.0 ** (-8.0 * np.arange(1, n_heads + 1) / n_heads), dtype=jnp.float32)


def mixer_dsa(x, w_in, w_out):
    Bn, S, _ = x.shape
    proj = x @ w_in
    q, k, v, qi, ki, wi = jnp.split(proj, DSA_SPLITS, axis=-1)
    q = q.reshape(Bn, S, DSA_KV_HEADS, DSA_REP, HEAD_DIM)
    k = k.reshape(Bn, S, DSA_KV_HEADS, HEAD_DIM)
    v = v.reshape(Bn, S, DSA_KV_HEADS, HEAD_DIM)
    qi = qi.reshape(Bn, S, IDX_HEADS, IDX_DIM)
    topk = min(TOPK_MAX, S // 4)
    nb = S // Q_BLOCK
    spos = jnp.arange(S)
    slopes = alibi_slopes(DSA_HEADS).reshape(DSA_KV_HEADS, DSA_REP)
    idx_scale = (IDX_DIM ** -0.5) * (IDX_HEADS ** -0.5)

    def block(args):
        q_b, qi_b, wi_b, bid = args
        tpos = bid * Q_BLOCK + jnp.arange(Q_BLOCK)
        rel = jax.nn.relu(jnp.einsum('bqhd,bsd->bqsh', qi_b, ki).astype(jnp.float32))
        score = jnp.einsum('bqsh,bqh->bqs', rel, wi_b.astype(jnp.float32)) * idx_scale
        causal = (spos[None, :] <= tpos[:, None])[None]
        score = jnp.where(causal, score, -jnp.inf)
        _, idx = lax.top_k(score, topk)
        k_sel = jax.vmap(lambda kb, ib: kb[ib])(k, idx)
        v_sel = jax.vmap(lambda vb, ib: vb[ib])(v, idx)
        logits = jnp.einsum('bqgrd,bqkgd->bqgrk', q_b, k_sel).astype(jnp.float32) * (HEAD_DIM ** -0.5)
        dist = (tpos[None, :, None] - idx).astype(jnp.float32)
        logits = logits - slopes[None, None, :, :, None] * dist[:, :, None, None, :]
        valid = (idx <= tpos[None, :, None])[:, :, None, None, :]
        logits = jnp.where(valid, logits, -jnp.inf)
        p = jax.nn.softmax(logits, axis=-1).astype(v.dtype)
        o = jnp.einsum('bqgrk,bqkgd->bqgrd', p, v_sel)
        return o.reshape(Bn, Q_BLOCK, DSA_WIDTH)

    out = lax.map(block, (to_blocks(q, nb), to_blocks(qi, nb), to_blocks(wi, nb), jnp.arange(nb)))
    out = jnp.moveaxis(out, 0, 1).reshape(Bn, S, DSA_WIDTH)
    return out @ w_out


def sq_relu_mlp(x, w1, w2):
    return jnp.square(jax.nn.relu(x @ w1)) @ w2


def setup_inputs(seed: int = 0) -> dict:
    key = jax.random.key(seed)
    ks = jax.random.split(key, 32)
    f32 = jnp.float32

    def nrm(k, shape, scale):
        return jax.random.normal(k, shape, f32) * scale

    ws = D_MODEL ** -0.5
    x = nrm(ks[0], (BATCH, SEQ, D_MODEL), 1.0)
    sb_ssm_w_in = jnp.concatenate([
        nrm(ks[1], (N_EVEN, D_MODEL, 2 * SB_WIDTH), ws),
        nrm(ks[2], (N_EVEN, D_MODEL, SB_WIDTH), ws * BETA),
        nrm(ks[3], (N_EVEN, D_MODEL, SSM_WIDTH), ws)], axis=-1)
    ssm_log_dt = jax.random.uniform(ks[4], (N_EVEN, SSM_GROUPS), f32,
                                    minval=math.log(1e-3), maxval=math.log(1e-1))
    n = jnp.arange(SSM_STATE, dtype=f32)
    ssm_lam_re = -0.5 + nrm(ks[5], (N_EVEN, SSM_GROUPS, SSM_STATE), 0.01)
    ssm_lam_im = math.pi * n + nrm(ks[6], (N_EVEN, SSM_GROUPS, SSM_STATE), 0.01)
    bsc = (2 * SSM_GROUP) ** -0.5
    ssm_b_re = nrm(ks[7], (N_EVEN, SSM_GROUPS, SSM_STATE, SSM_GROUP), bsc)
    ssm_b_im = nrm(ks[8], (N_EVEN, SSM_GROUPS, SSM_STATE, SSM_GROUP), bsc)
    csc = (2 * SSM_STATE) ** -0.5
    ssm_c_re = nrm(ks[9], (N_EVEN, SSM_GROUPS, SSM_GROUP, SSM_STATE), csc)
    ssm_c_im = nrm(ks[10], (N_EVEN, SSM_GROUPS, SSM_GROUP, SSM_STATE), csc)
    ssm_d = nrm(ks[11], (N_EVEN, SSM_GROUPS, SSM_GROUP), 1.0)
    ssm_w_glu = nrm(ks[12], (N_EVEN, SSM_WIDTH, SSM_WIDTH), SSM_WIDTH ** -0.5)
    ssm_b_glu = nrm(ks[13], (N_EVEN, SSM_WIDTH), 0.02)
    sb_ssm_w_out = nrm(ks[14], (N_EVEN, MIX_WIDTH, D_MODEL), (MIX_WIDTH ** -0.5) * BETA)
    kvw = DSA_KV_HEADS * HEAD_DIM
    dsa_w_in = jnp.concatenate([
        nrm(ks[15], (N_ODD, D_MODEL, DSA_WIDTH + kvw), ws),
        nrm(ks[16], (N_ODD, D_MODEL, kvw), ws * BETA),
        nrm(ks[17], (N_ODD, D_MODEL, IDX_HEADS * IDX_DIM + IDX_DIM + IDX_HEADS), ws)], axis=-1)
    dsa_w_out = nrm(ks[18], (N_ODD, DSA_WIDTH, D_MODEL), (DSA_WIDTH ** -0.5) * BETA)
    ln_mix_g = 1.0 + nrm(ks[19], (DEPTH, D_MODEL), 0.02)
    ln_mix_b = nrm(ks[20], (DEPTH, D_MODEL), 0.02)
    ln_ffn_g = 1.0 + nrm(ks[21], (DEPTH, D_MODEL), 0.02)
    ln_ffn_b = nrm(ks[22], (DEPTH, D_MODEL), 0.02)
    mlp_w1 = nrm(ks[23], (DEPTH, D_MODEL, D_FF), ws * BETA)
    mlp_w2 = nrm(ks[24], (DEPTH, D_FF, D_MODEL), (D_FF ** -0.5) * BETA)
    return {"x": x, "sb_ssm_w_in": sb_ssm_w_in, "ssm_log_dt": ssm_log_dt,
            "ssm_lam_re": ssm_lam_re, "ssm_lam_im": ssm_lam_im,
            "ssm_b_re": ssm_b_re, "ssm_b_im": ssm_b_im,
            "ssm_c_re": ssm_c_re, "ssm_c_im": ssm_c_im, "ssm_d": ssm_d,
            "ssm_w_glu": ssm_w_glu, "ssm_b_glu": ssm_b_glu, "sb_ssm_w_out": sb_ssm_w_out,
            "dsa_w_in": dsa_w_in, "dsa_w_out": dsa_w_out,
            "ln_mix_g": ln_mix_g, "ln_mix_b": ln_mix_b,
            "ln_ffn_g": ln_ffn_g, "ln_ffn_b": ln_ffn_b,
            "mlp_w1": mlp_w1, "mlp_w2": mlp_w2}


def reference(x, sb_ssm_w_in, ssm_log_dt, ssm_lam_re, ssm_lam_im, ssm_b_re, ssm_b_im,
              ssm_c_re, ssm_c_im, ssm_d, ssm_w_glu, ssm_b_glu, sb_ssm_w_out,
              dsa_w_in, dsa_w_out, ln_mix_g, ln_mix_b, ln_ffn_g, ln_ffn_b, mlp_w1, mlp_w2):
    for i in range(DEPTH):
        j = i // 2
        if i % 2 == 0:
            h = mixer_sb_ssm(x, sb_ssm_w_in[j], ssm_log_dt[j], ssm_lam_re[j], ssm_lam_im[j],
                             ssm_b_re[j], ssm_b_im[j], ssm_c_re[j], ssm_c_im[j], ssm_d[j],
                             ssm_w_glu[j], ssm_b_glu[j], sb_ssm_w_out[j])
        else:
            h = mixer_dsa(x, dsa_w_in[j], dsa_w_out[j])
        x = layer_norm(ALPHA * x + h, ln_mix_g[i], ln_mix_b[i])
        x = layer_norm(ALPHA * x + sq_relu_mlp(x, mlp_w1[i], mlp_w2[i]), ln_ffn_g[i], ln_ffn_b[i])
    return x
```

```python
import functools
import math

import jax
import jax.numpy as jnp
import numpy as np
from jax import lax
from jax.experimental import pallas as pl
from jax.experimental.pallas import tpu as pltpu

F32 = jnp.float32
BF16 = jnp.bfloat16
I32 = jnp.int32

D_MODEL = 1024
HEAD_DIM = 64
SB_WIDTH = 512
SB_HEADS = SB_WIDTH // HEAD_DIM
SSM_WIDTH = 512
SSM_GROUP = 16
SSM_GROUPS = SSM_WIDTH // SSM_GROUP
SSM_STATE = 64
DSA_HEADS = 16
DSA_KV_HEADS = 4
DSA_REP = DSA_HEADS // DSA_KV_HEADS
IDX_HEADS = 8
IDX_DIM = 32
TOPK_MAX = 256
D_FF = 4 * D_MODEL
DEPTH = 4
ALPHA = (2 * DEPTH) ** 0.25
LN_EPS = 1e-5
IDX_SCALE = (IDX_DIM ** -0.5) * (IDX_HEADS ** -0.5)
QK_SCALE = HEAD_DIM ** -0.5

SSM_CHUNK = 16
SUBLANES = 8
LANES = 128
VMEM_LIMIT = 56 * 1024 * 1024
EXP_UNDERFLOW = -104.0
NEG_BIG = -1e30
INT_MIN = -2 ** 31

_NT = (((1,), (1,)), ((), ()))


def _params(*sem):
    return pltpu.CompilerParams(dimension_semantics=sem, vmem_limit_bytes=VMEM_LIMIT)


def _layer_norm(r, g, b):
    mu = jnp.mean(r, axis=-1, keepdims=True)
    c = r - mu
    var = jnp.mean(c * c, axis=-1, keepdims=True)
    return c * lax.rsqrt(var + LN_EPS) * g + b


def _proj_kernel(x_ref, w_ref, o_ref):
    o_ref[...] = jnp.dot(x_ref[...].astype(BF16), w_ref[...],
                         preferred_element_type=F32).astype(o_ref.dtype)


def _proj(x2d, w, tm=512):
    m, k = x2d.shape
    n = w.shape[1]
    return pl.pallas_call(
        _proj_kernel,
        grid=(m // tm,),
        in_specs=[pl.BlockSpec((tm, k), lambda i: (i, 0)),
                  pl.BlockSpec((k, n), lambda i: (0, 0))],
        out_specs=pl.BlockSpec((tm, n), lambda i: (i, 0)),
        out_shape=jax.ShapeDtypeStruct((m, n), BF16),
        compiler_params=_params("parallel"),
        name="proj",
    )(x2d, w)


def _sb_kernel(q_ref, k_ref, vt_ref, o_ref, acc_ref, a_ref, *, tq, tk):
    i = pl.program_id(1)
    heads = q_ref.shape[1]
    acc_ref[...] = jnp.zeros_like(acc_ref)
    a_ref[...] = jnp.zeros_like(a_ref)
    rows = lax.broadcasted_iota(I32, (tk, tq), 0)
    tcol = i * tq + lax.broadcasted_iota(I32, (tk, tq), 1)
    tri = lax.broadcasted_iota(I32, (tk, tk), 1) > lax.broadcasted_iota(I32, (tk, tk), 0)
    tri = jnp.where(tri, 1.0, 0.0).astype(BF16)
    tri2 = jnp.concatenate([tri, tri], axis=1)

    def body(carry):
        j, _ = carry
        k0 = pl.multiple_of(j * tk, tk)
        mask = (k0 + rows) < tcol
        amax = None
        for h in range(heads):
            kb = k_ref[0, h, pl.ds(k0, tk), :]
            qh = q_ref[0, h] * QK_SCALE
            z = lax.dot_general(kb, qh, _NT, preferred_element_type=F32)
            sp = jnp.log1p(jnp.exp(-jnp.abs(z)))
            log_beta = jnp.minimum(z, 0.0) - sp
            l1 = jnp.where(mask, -jnp.maximum(z, 0.0) - sp, 0.0)
            hi = l1.astype(BF16)
            lo = (l1 - hi.astype(F32)).astype(BF16)
            e = jnp.dot(tri2, jnp.concatenate([hi, lo], axis=0), preferred_element_type=F32)
            a = a_ref[h]
            w = jnp.where(mask, jnp.exp(log_beta + a + e), 0.0).astype(BF16)
            acc_ref[h] += jnp.dot(vt_ref[0, h, :, pl.ds(k0, tk)], w, preferred_element_type=F32)
            a_new = a + jnp.sum(l1, axis=0, keepdims=True)
            a_ref[h] = a_new
            amax = a_new if amax is None else jnp.maximum(amax, a_new)
        return j - 1, jnp.max(amax) > EXP_UNDERFLOW

    j0 = ((i + 1) * tq - 1) // tk
    lax.while_loop(lambda c: jnp.logical_and(c[0] >= 0, c[1]), body, (j0, True))
    o_ref[0] = acc_ref[...].astype(o_ref.dtype)


def _sb_attention(q, k, vt, tq=128, tk=128):
    b, h, s, dh = q.shape
    return pl.pallas_call(
        functools.partial(_sb_kernel, tq=tq, tk=tk),
        grid=(b, s // tq),
        in_specs=[pl.BlockSpec((1, h, tq, dh), lambda bi, i: (bi, 0, i, 0)),
                  pl.BlockSpec((1, h, s, dh), lambda bi, i: (bi, 0, 0, 0)),
                  pl.BlockSpec((1, h, dh, s), lambda bi, i: (bi, 0, 0, 0))],
        out_specs=pl.BlockSpec((1, h, dh, tq), lambda bi, i: (bi, 0, 0, i)),
        out_shape=jax.ShapeDtypeStruct((b, h, dh, s), BF16),
        scratch_shapes=[pltpu.VMEM((h, dh, tq), F32), pltpu.VMEM((h, 1, tq), F32)],
        compiler_params=_params("parallel", "parallel"),
        name="sb_attention",
    )(q, k, vt)


def _ssm_kmat_kernel(l_ref, b_ref, o_ref):
    for g in range(l_ref.shape[0]):
        o_ref[g] = jnp.dot(l_ref[g], b_ref[g], precision=lax.Precision.HIGHEST,
                           preferred_element_type=F32)


def _ssm_kmat(lcat, bcat, gb=8):
    g, r, n2 = lcat.shape
    c = bcat.shape[2]
    return pl.pallas_call(
        _ssm_kmat_kernel,
        grid=(g // gb,),
        in_specs=[pl.BlockSpec((gb, r, n2), lambda i: (i, 0, 0)),
                  pl.BlockSpec((gb, n2, c), lambda i: (i, 0, 0))],
        out_specs=pl.BlockSpec((gb, r, c), lambda i: (i, 0, 0)),
        out_shape=jax.ShapeDtypeStruct((g, r, c), F32),
        compiler_params=_params("parallel"),
        name="ssm_kmat",
    )(lcat, bcat)


def _cmul(ar, ai, xr, xi):
    return ar * xr - ai * xi, ar * xi + ai * xr


def _ssm_kernel(u_ref, pt_ref, qr_ref, qi_ref, rr_ref, ri_ref, al_ref, aseg_ref, d_ref, y_ref,
                xr_ref, xi_ref, *, gb, nch, nseg, batch):
    rows = nch * SUBLANES
    for g in range(gb):
        ub = u_ref[g]
        xr_ref[g] = jnp.dot(ub, qr_ref[g], preferred_element_type=F32).reshape(nch, SUBLANES, SSM_STATE)
        xi_ref[g] = jnp.dot(ub, qi_ref[g], preferred_element_type=F32).reshape(nch, SUBLANES, SSM_STATE)

    def scan_body(c, carry):
        new = []
        for g in range(gb):
            sr, si = carry[g]
            inr = xr_ref[g, c]
            ini = xi_ref[g, c]
            xr_ref[g, c] = sr
            xi_ref[g, c] = si
            pr, pi = _cmul(al_ref[g, 0:1, :], al_ref[g, 1:2, :], sr, si)
            new.append((pr + inr, pi + ini))
        return tuple(new)

    zero = jnp.zeros((SUBLANES, SSM_STATE), F32)
    ends = lax.fori_loop(0, nch, scan_body, tuple((zero, zero) for _ in range(gb)))

    if nseg > 1:
        row = lax.broadcasted_iota(I32, (SUBLANES, SSM_STATE), 0)
        cins = []
        for g in range(gb):
            er, ei = ends[g]
            cr, ci = zero, zero
            for _ in range(nseg - 1):
                pr, pi = _cmul(aseg_ref[g, 0:1, :], aseg_ref[g, 1:2, :], cr, ci)
                cr = jnp.where(row >= batch, pltpu.roll(er + pr, batch, 0), 0.0)
                ci = jnp.where(row >= batch, pltpu.roll(ei + pi, batch, 0), 0.0)
            cins.append((cr, ci))

        def corr_body(c, carry):
            new = []
            for g in range(gb):
                cr, ci = carry[g]
                xr_ref[g, c] += cr
                xi_ref[g, c] += ci
                new.append(_cmul(al_ref[g, 0:1, :], al_ref[g, 1:2, :], cr, ci))
            return tuple(new)

        lax.fori_loop(0, nch, corr_body, tuple(cins))

    for g in range(gb):
        ub = u_ref[g]
        xcr = xr_ref[g].reshape(rows, SSM_STATE).astype(BF16)
        xci = xi_ref[g].reshape(rows, SSM_STATE).astype(BF16)
        y = jnp.dot(ub, pt_ref[g], preferred_element_type=F32)
        y += jnp.dot(xcr, rr_ref[g], preferred_element_type=F32)
        y += jnp.dot(xci, ri_ref[g], preferred_element_type=F32)
        y_ref[g] = y + ub.astype(F32) * d_ref[g]


def _ssm_scan(u_g, pt, qr, qi, rr, ri, al, aseg, d_t, *, nch, nseg, batch, gb=4):
    g, rows, w = u_g.shape
    n = SSM_STATE
    full3 = lambda a, b: pl.BlockSpec((gb, a, b), lambda i: (i, 0, 0))
    return pl.pallas_call(
        functools.partial(_ssm_kernel, gb=gb, nch=nch, nseg=nseg, batch=batch),
        grid=(g // gb,),
        in_specs=[full3(rows, w), full3(w, w), full3(w, n), full3(w, n), full3(n, w), full3(n, w),
                  full3(2, n), full3(2, n), full3(1, w)],
        out_specs=full3(rows, w),
        out_shape=jax.ShapeDtypeStruct((g, rows, w), F32),
        scratch_shapes=[pltpu.VMEM((gb, nch, SUBLANES, n), F32), pltpu.VMEM((gb, nch, SUBLANES, n), F32)],
        compiler_params=_params("parallel"),
        name="ssm_scan",
    )(u_g, pt, qr, qi, rr, ri, al, aseg, d_t)


def _ssm(u, log_dt, lam_re, lam_im, b_re, b_im, c_re, c_im, d):
    bsz, s, _ = u.shape
    L, G, C, N = SSM_CHUNK, SSM_GROUPS, SSM_GROUP, SSM_STATE
    nseg = SUBLANES // bsz
    nch = s // (L * nseg)
    dt = jnp.exp(log_dt.astype(F32))[:, None]
    lr = lam_re.astype(F32)
    li = lam_im.astype(F32)

    def apow(tau):
        tau = jnp.asarray(tau, F32)[..., None, None]
        mag = jnp.exp(tau * (lr * dt))
        ang = tau * (li * dt)
        return mag * jnp.cos(ang), mag * jnp.sin(ang)

    pr, pi = apow(np.arange(L + 1))
    den = lr * lr + li * li
    nr = pr[1] - 1.0
    coef_r = ((nr * lr + pi[1] * li) / den)[..., None]
    coef_i = ((pi[1] * lr - nr * li) / den)[..., None]
    br = b_re.astype(F32)
    bi = b_im.astype(F32)
    bbar_r = coef_r * br - coef_i * bi
    bbar_i = coef_r * bi + coef_i * br
    cr = c_re.astype(F32)
    ci = c_im.astype(F32)
    car = cr[None] * pr[:, :, None, :] - ci[None] * pi[:, :, None, :]
    cai = cr[None] * pi[:, :, None, :] + ci[None] * pr[:, :, None, :]
    lcat = jnp.concatenate([car[:L], -cai[:L]], axis=-1)
    lcat = lcat.transpose(1, 0, 2, 3).reshape(G, L * C, 2 * N)
    bcat = jnp.concatenate([bbar_r, bbar_i], axis=1)
    kmat = _ssm_kmat(lcat, bcat).reshape(G, L, C, C)
    tt = np.arange(L)[:, None]
    ss = np.arange(L)[None, :]
    kg = kmat[:, np.clip(tt - ss, 0, None)]
    kg = jnp.where((tt >= ss)[None, :, :, None, None], kg, 0.0)
    pt = kg.transpose(0, 2, 4, 1, 3).reshape(G, L * C, L * C).astype(BF16)
    prq = pr[L - 1 - np.arange(L)]
    piq = pi[L - 1 - np.arange(L)]
    qr = prq[..., None] * bbar_r[None] - piq[..., None] * bbar_i[None]
    qi = prq[..., None] * bbar_i[None] + piq[..., None] * bbar_r[None]
    qr = qr.transpose(1, 0, 3, 2).reshape(G, L * C, N).astype(BF16)
    qi = qi.transpose(1, 0, 3, 2).reshape(G, L * C, N).astype(BF16)
    rr = car[1:].transpose(1, 3, 0, 2).reshape(G, N, L * C).astype(BF16)
    ri = (-cai[1:]).transpose(1, 3, 0, 2).reshape(G, N, L * C).astype(BF16)
    al = jnp.stack([pr[L], pi[L]], axis=1)
    sr, si = apow(np.asarray([L * nch]))
    aseg = jnp.stack([sr[0], si[0]], axis=1)
    d_t = jnp.tile(d.astype(F32), (1, L))[:, None, :]

    u_g = u.reshape(bsz, nseg, nch, L, G, C).transpose(4, 2, 1, 0, 3, 5).reshape(G, nch * nseg * bsz, L * C)
    y_g = _ssm_scan(u_g, pt, qr, qi, rr, ri, al, aseg, d_t, nch=nch, nseg=nseg, batch=bsz)
    y = y_g.reshape(G, nch, nseg, bsz, L, C).transpose(3, 2, 1, 4, 0, 5)
    return y.reshape(bsz * s, G * C)


def _mix_even_kernel(a_ref, y_ref, x_ref, wglu_ref, bglu_ref, wo_ref, g_ref, b_ref, o_ref):
    y = jax.nn.gelu(y_ref[...], approximate=True)
    gate = jnp.dot(y.astype(BF16), wglu_ref[...], preferred_element_type=F32) + bglu_ref[...]
    y = y * (1.0 / (1.0 + jnp.exp(-gate)))
    h = jnp.dot(a_ref[...], wo_ref[:SB_WIDTH, :], preferred_element_type=F32)
    h += jnp.dot(y.astype(BF16), wo_ref[SB_WIDTH:, :], preferred_element_type=F32)
    o_ref[...] = _layer_norm(ALPHA * x_ref[...] + h, g_ref[...], b_ref[...])


def _mix_even(a_out, y, x2d, w_glu, b_glu, w_out, g, b, tm=512):
    m, d = x2d.shape
    row = lambda w: pl.BlockSpec((tm, w), lambda i: (i, 0))
    const = lambda r, c: pl.BlockSpec((r, c), lambda i: (0, 0))
    return pl.pallas_call(
        _mix_even_kernel,
        grid=(m // tm,),
        in_specs=[row(SB_WIDTH), row(SSM_WIDTH), row(d), const(SSM_WIDTH, SSM_WIDTH), const(1, SSM_WIDTH),
                  const(d, d), const(1, d), const(1, d)],
        out_specs=row(d),
        out_shape=jax.ShapeDtypeStruct((m, d), F32),
        compiler_params=_params("parallel"),
        name="mix_even",
    )(a_out, y, x2d, w_glu, b_glu, w_out, g, b)


def _mix_odd_kernel(o_in_ref, x_ref, wo_ref, g_ref, b_ref, o_ref):
    h = jnp.dot(o_in_ref[...], wo_ref[...], preferred_element_type=F32)
    o_ref[...] = _layer_norm(ALPHA * x_ref[...] + h, g_ref[...], b_ref[...])


def _mix_odd(o_in, x2d, w_out, g, b, tm=512):
    m, d = x2d.shape
    row = lambda w: pl.BlockSpec((tm, w), lambda i: (i, 0))
    const = lambda r, c: pl.BlockSpec((r, c), lambda i: (0, 0))
    return pl.pallas_call(
        _mix_odd_kernel,
        grid=(m // tm,),
        in_specs=[row(d), row(d), const(d, d), const(1, d), const(1, d)],
        out_specs=row(d),
        out_shape=jax.ShapeDtypeStruct((m, d), F32),
        compiler_params=_params("parallel"),
        name="mix_odd",
    )(o_in, x2d, w_out, g, b)


def _mlp_kernel(x_ref, w1_ref, w2_ref, g_ref, b_ref, o_ref, *, fchunk):
    x = x_ref[...]
    xb = x.astype(BF16)
    acc = jnp.zeros(x.shape, F32)
    for f in range(0, w1_ref.shape[1], fchunk):
        h = jnp.dot(xb, w1_ref[:, f:f + fchunk], preferred_element_type=F32)
        h = jnp.maximum(h, 0.0)
        acc += jnp.dot((h * h).astype(BF16), w2_ref[f:f + fchunk, :], preferred_element_type=F32)
    o_ref[...] = _layer_norm(ALPHA * x + acc, g_ref[...], b_ref[...])


def _mlp(x2d, w1, w2, g, b, tm=512, fchunk=1024):
    m, d = x2d.shape
    f = w1.shape[1]
    row = pl.BlockSpec((tm, d), lambda i: (i, 0))
    once = lambda r, c: pl.BlockSpec((r, c), lambda i: (0, 0), pipeline_mode=pl.Buffered(1))
    return pl.pallas_call(
        functools.partial(_mlp_kernel, fchunk=fchunk),
        grid=(m // tm,),
        in_specs=[row, once(d, f), once(f, d), once(1, d), once(1, d)],
        out_specs=row,
        out_shape=jax.ShapeDtypeStruct((m, d), F32),
        compiler_params=_params("parallel"),
        name="mlp",
    )(x2d, w1, w2, g, b)


def _alibi_slopes():
    return [float(2.0 ** (-8.0 * (h + 1) / DSA_HEADS)) for h in range(DSA_HEADS)]


def _dsa_kernel(q_ref, k_ref, v_ref, qi_ref, ki_ref, wit_ref, o_ref, key_ref, m_ref, acc_ref,
                *, tq, tkb, tk, topk):
    i = pl.program_id(1)
    t0 = i * tq
    n_idx_heads = qi_ref.shape[1]
    kv_heads = k_ref.shape[1]
    rep = q_ref.shape[2]
    dh = k_ref.shape[3]

    nsb = (t0 + tq + tkb - 1) // tkb
    srow = lax.broadcasted_iota(I32, (tkb, tq), 0)
    tcol = t0 + lax.broadcasted_iota(I32, (tkb, tq), 1)
    wt = wit_ref[0] * IDX_SCALE

    def score_body(jb, carry):
        s0 = pl.multiple_of(jb * tkb, tkb)
        kib = ki_ref[0, pl.ds(s0, tkb), :]
        sc = jnp.zeros((tkb, tq), F32)
        for h in range(n_idx_heads):
            z = lax.dot_general(kib, qi_ref[0, h], _NT, preferred_element_type=F32)
            sc += wt[h:h + 1, :] * jnp.maximum(z, 0.0)
        sc = jnp.where(sc == 0.0, 0.0, sc)
        sc = jnp.where(s0 + srow <= tcol, sc, -jnp.inf)
        bits = pltpu.bitcast(sc, I32)
        key_ref[pl.ds(s0, tkb), :] = bits ^ ((bits >> 31) & jnp.int32(0x7FFFFFFF))
        return carry

    lax.fori_loop(0, nsb, score_body, 0)

    def count(pred):
        def body(jb, acc):
            s0 = pl.multiple_of(jb * tkb, tkb)
            ind = jnp.where(pred(key_ref[pl.ds(s0, tkb), :], s0 + srow), 1.0, 0.0)
            return acc + jnp.sum(ind.reshape(tkb // SUBLANES, SUBLANES, tq), axis=0)
        acc = lax.fori_loop(0, nsb, body, jnp.zeros((SUBLANES, tq), F32))
        return jnp.sum(acc, axis=0, keepdims=True)

    kf = float(topk)
    c0 = count(lambda kb, sidx: kb >= 0)
    thr = jnp.where(c0 >= kf, jnp.int32(0), jnp.int32(INT_MIN))

    def thr_body(it, thr):
        cand = thr | (jnp.int32(1) << (30 - it))
        c = count(lambda kb, sidx: kb >= cand)
        return jnp.where(c >= kf, cand, thr)

    thr = lax.fori_loop(0, 31, thr_body, thr)
    n_gt = count(lambda kb, sidx: kb > thr)
    n_ge = count(lambda kb, sidx: kb >= thr)
    need = kf - n_gt
    has_ties = jnp.max(n_ge) > kf

    def cut_body(it, cut):
        cand = cut | (jnp.int32(1) << (12 - it))
        c = count(lambda kb, sidx: jnp.logical_and(kb == thr, sidx < cand))
        return jnp.where(c <= need, cand, cut)

    cut = lax.fori_loop(0, jnp.where(has_ties, 13, 0), cut_body, jnp.zeros((1, tq), I32))
    cut = jnp.where(has_ties, cut, jnp.int32(2 ** 30))

    m_ref[...] = jnp.full(m_ref.shape, NEG_BIG, F32)
    acc_ref[...] = jnp.zeros_like(acc_ref)
    nkb = (t0 + tq + tk - 1) // tk
    srow_k = lax.broadcasted_iota(I32, (tk, tq), 0)
    tcol_k = t0 + lax.broadcasted_iota(I32, (tk, tq), 1)
    slopes = _alibi_slopes()

    def attn_body(jb, carry):
        s0 = pl.multiple_of(jb * tk, tk)
        kb = key_ref[pl.ds(s0, tk), :]
        sidx = s0 + srow_k
        sel = jnp.logical_or(kb > thr, jnp.logical_and(kb == thr, sidx < cut))
        sel = jnp.logical_and(sel, sidx <= tcol_k)
        bias = jnp.where(sel, 0.0, NEG_BIG).T
        spos = (s0 + lax.broadcasted_iota(I32, (1, tk), 1)).astype(F32)
        for g in range(kv_heads):
            qg = q_ref[0, g].reshape(rep * tq, dh) * QK_SCALE
            kg = k_ref[0, g, pl.ds(s0, tk), :]
            lg = lax.dot_general(qg, kg, _NT, preferred_element_type=F32)
            lg = jnp.concatenate(
                [lg[r * tq:(r + 1) * tq] + (bias + slopes[g * rep + r] * spos) for r in range(rep)], axis=0)
            m_prev = m_ref[g]
            m_new = jnp.maximum(m_prev, jnp.max(lg, axis=1, keepdims=True))
            p = jnp.exp(lg - m_new)
            acc_ref[g] = jnp.exp(m_prev - m_new) * acc_ref[g] + jnp.dot(
                p.astype(BF16), v_ref[0, g, pl.ds(s0, tk), :], preferred_element_type=F32)
            m_ref[g] = m_new
        return carry

    lax.fori_loop(0, nkb, attn_body, 0)
    for g in range(kv_heads):
        acc = acc_ref[g]
        out = acc[:, :dh] / acc[:, dh:dh + 1]
        for r in range(rep):
            o_ref[0, g * rep + r] = out[r * tq:(r + 1) * tq].astype(o_ref.dtype)


def _dsa_attention(q, k, v_ext, qi, ki, wit, topk, tq=128, tkb=512, tk=512):
    b, kv, rep, s, dh = q.shape
    ih, di = qi.shape[1], qi.shape[3]
    tkb = min(tkb, s)
    tk = min(tk, s)
    return pl.pallas_call(
        functools.partial(_dsa_kernel, tq=tq, tkb=tkb, tk=tk, topk=topk),
        grid=(b, s // tq),
        in_specs=[pl.BlockSpec((1, kv, rep, tq, dh), lambda bi, i: (bi, 0, 0, i, 0)),
                  pl.BlockSpec((1, kv, s, dh), lambda bi, i: (bi, 0, 0, 0)),
                  pl.BlockSpec((1, kv, s, 2 * dh), lambda bi, i: (bi, 0, 0, 0)),
                  pl.BlockSpec((1, ih, tq, di), lambda bi, i: (bi, 0, i, 0)),
                  pl.BlockSpec((1, s, di), lambda bi, i: (bi, 0, 0)),
                  pl.BlockSpec((1, ih, tq), lambda bi, i: (bi, 0, i))],
        out_specs=pl.BlockSpec((1, kv * rep, tq, dh), lambda bi, i: (bi, 0, i, 0)),
        out_shape=jax.ShapeDtypeStruct((b, kv * rep, s, dh), BF16),
        scratch_shapes=[pltpu.VMEM((s, tq), I32),
                        pltpu.VMEM((kv, rep * tq, 1), F32),
                        pltpu.VMEM((kv, rep * tq, 2 * dh), F32)],
        compiler_params=_params("parallel", "parallel"),
        name="dsa_attention",
    )(q, k, v_ext, qi, ki, wit)


def _even_layer(x2d, bsz, s, w_in, ssm_params, w_glu, b_glu, w_out, g_mix, b_mix):
    proj = _proj(x2d, w_in.astype(BF16)).reshape(bsz, s, -1)
    heads = lambda a: a.reshape(bsz, s, SB_HEADS, HEAD_DIM)
    q = heads(proj[..., :SB_WIDTH]).transpose(0, 2, 1, 3)
    k = heads(proj[..., SB_WIDTH:2 * SB_WIDTH]).transpose(0, 2, 1, 3)
    vt = heads(proj[..., 2 * SB_WIDTH:3 * SB_WIDTH]).transpose(0, 2, 3, 1)
    u = proj[..., 3 * SB_WIDTH:]
    a_t = _sb_attention(q, k, vt)
    a_out = a_t.transpose(0, 3, 1, 2).reshape(bsz * s, SB_WIDTH)
    y = _ssm(u, *ssm_params)
    return _mix_even(a_out, y, x2d, w_glu.astype(BF16), b_glu.astype(F32)[None], w_out.astype(BF16),
                     g_mix.astype(F32)[None], b_mix.astype(F32)[None])


def _odd_layer(x2d, bsz, s, w_in, w_out, g_mix, b_mix):
    proj = _proj(x2d, w_in.astype(BF16)).reshape(bsz, s, -1)
    kvw = DSA_KV_HEADS * HEAD_DIM
    o0 = DSA_HEADS * HEAD_DIM
    q = proj[..., :o0].reshape(bsz, s, DSA_KV_HEADS, DSA_REP, HEAD_DIM).transpose(0, 2, 3, 1, 4)
    k = proj[..., o0:o0 + kvw].reshape(bsz, s, DSA_KV_HEADS, HEAD_DIM).transpose(0, 2, 1, 3)
    v = proj[..., o0 + kvw:o0 + 2 * kvw].reshape(bsz, s, DSA_KV_HEADS, HEAD_DIM).transpose(0, 2, 1, 3)
    ones = jnp.ones(v.shape[:-1] + (1,), BF16)
    zeros = jnp.zeros(v.shape[:-1] + (HEAD_DIM - 1,), BF16)
    v_ext = jnp.concatenate([v, ones, zeros], axis=-1)
    o1 = o0 + 2 * kvw
    qi = proj[..., o1:o1 + IDX_HEADS * IDX_DIM].reshape(bsz, s, IDX_HEADS, IDX_DIM).transpose(0, 2, 1, 3)
    o2 = o1 + IDX_HEADS * IDX_DIM
    ki = proj[..., o2:o2 + IDX_DIM]
    wit = proj[..., o2 + IDX_DIM:o2 + IDX_DIM + IDX_HEADS].astype(F32).transpose(0, 2, 1)
    o = _dsa_attention(q, k, v_ext, qi, ki, wit, min(TOPK_MAX, s // 4))
    o = o.transpose(0, 2, 1, 3).reshape(bsz * s, DSA_HEADS * HEAD_DIM)
    return _mix_odd(o, x2d, w_out.astype(BF16), g_mix.astype(F32)[None], b_mix.astype(F32)[None])


def kernel(x, sb_ssm_w_in, ssm_log_dt, ssm_lam_re, ssm_lam_im, ssm_b_re, ssm_b_im, ssm_c_re, ssm_c_im, ssm_d,
           ssm_w_glu, ssm_b_glu, sb_ssm_w_out, dsa_w_in, dsa_w_out, ln_mix_g, ln_mix_b, ln_ffn_g, ln_ffn_b,
           mlp_w1, mlp_w2):
    bsz, s, d = x.shape
    x2d = x.reshape(bsz * s, d).astype(F32)
    depth = mlp_w1.shape[0]
    for i in range(depth):
        j = i // 2
        if i % 2 == 0:
            ssm_params = (ssm_log_dt[j], ssm_lam_re[j], ssm_lam_im[j], ssm_b_re[j], ssm_b_im[j],
                          ssm_c_re[j], ssm_c_im[j], ssm_d[j])
            x2d = _even_layer(x2d, bsz, s, sb_ssm_w_in[j], ssm_params, ssm_w_glu[j], ssm_b_glu[j],
                              sb_ssm_w_out[j], ln_mix_g[i], ln_mix_b[i])
        else:
            n_in = dsa_w_in.shape[2]
            pad = (-n_in) % LANES
            w_in = jnp.pad(dsa_w_in[j], ((0, 0), (0, pad)))
            x2d = _odd_layer(x2d, bsz, s, w_in, dsa_w_out[j], ln_mix_g[i], ln_mix_b[i])
        x2d = _mlp(x2d, mlp_w1[i].astype(BF16), mlp_w2[i].astype(BF16),
                   ln_ffn_g[i].astype(F32)[None], ln_ffn_b[i].astype(F32)[None])
    return x2d.reshape(bsz, s, d).astype(x.dtype)
```

```python
import functools
import math

import jax
import jax.numpy as jnp
import numpy as np
from jax import lax
from jax.experimental import pallas as pl
from jax.experimental.pallas import tpu as pltpu

F32 = jnp.float32
BF16 = jnp.bfloat16
I32 = jnp.int32

D_MODEL = 1024
HEAD_DIM = 64
SB_WIDTH = 512
SB_HEADS = SB_WIDTH // HEAD_DIM
SSM_WIDTH = 512
SSM_GROUP = 16
SSM_GROUPS = SSM_WIDTH // SSM_GROUP
SSM_STATE = 64
DSA_HEADS = 16
DSA_KV_HEADS = 4
DSA_REP = DSA_HEADS // DSA_KV_HEADS
IDX_HEADS = 8
IDX_DIM = 32
TOPK_MAX = 256
D_FF = 4 * D_MODEL
DEPTH = 4
ALPHA = (2 * DEPTH) ** 0.25
LN_EPS = 1e-5
IDX_SCALE = (IDX_DIM ** -0.5) * (IDX_HEADS ** -0.5)
QK_SCALE = HEAD_DIM ** -0.5

SSM_CHUNK = 16
SUBLANES = 8
BF16_SUBLANES = 16
LANES = 128
VMEM_LIMIT = 56 * 1024 * 1024
EXP_UNDERFLOW = -104.0
NEG_BIG = -1e30
INT_MIN = -2 ** 31
COUNT_ROWS = 64

_NT = (((1,), (1,)), ((), ()))


def _params(*sem):
    return pltpu.CompilerParams(dimension_semantics=sem, vmem_limit_bytes=VMEM_LIMIT)


def _layer_norm(r, g, b):
    mu = jnp.mean(r, axis=-1, keepdims=True)
    c = r - mu
    var = jnp.mean(c * c, axis=-1, keepdims=True)
    return c * lax.rsqrt(var + LN_EPS) * g + b


def _proj_kernel(x_ref, w_ref, o_ref):
    o_ref[...] = jnp.dot(x_ref[...].astype(BF16), w_ref[...],
                         preferred_element_type=F32).astype(o_ref.dtype)


def _proj(x2d, w, tm=512):
    m, k = x2d.shape
    n = w.shape[1]
    return pl.pallas_call(
        _proj_kernel,
        grid=(m // tm,),
        in_specs=[pl.BlockSpec((tm, k), lambda i: (i, 0)),
                  pl.BlockSpec((k, n), lambda i: (0, 0))],
        out_specs=pl.BlockSpec((tm, n), lambda i: (i, 0)),
        out_shape=jax.ShapeDtypeStruct((m, n), BF16),
        compiler_params=_params("parallel"),
        name="proj",
    )(x2d, w)


def _sb_kernel(q_ref, k_ref, vt_ref, o_ref, acc_ref, a_ref, *, tq, tk):
    i = pl.program_id(1)
    heads = q_ref.shape[1]
    acc_ref[...] = jnp.zeros_like(acc_ref)
    a_ref[...] = jnp.zeros_like(a_ref)
    rows = lax.broadcasted_iota(I32, (tk, tq), 0)
    tcol = i * tq + lax.broadcasted_iota(I32, (tk, tq), 1)
    tri = lax.broadcasted_iota(I32, (tk, tk), 1) > lax.broadcasted_iota(I32, (tk, tk), 0)
    tri = jnp.where(tri, 1.0, 0.0).astype(BF16)
    tri2 = jnp.concatenate([tri, tri], axis=1)

    def body(carry):
        j, _ = carry
        k0 = pl.multiple_of(j * tk, tk)
        mask = (k0 + rows) < tcol
        amax = None
        for h in range(heads):
            kb = k_ref[0, h, pl.ds(k0, tk), :]
            qh = q_ref[0, h] * QK_SCALE
            z = lax.dot_general(kb, qh, _NT, preferred_element_type=F32)
            sp = jnp.log1p(jnp.exp(-jnp.abs(z)))
            log_beta = jnp.minimum(z, 0.0) - sp
            l1 = jnp.where(mask, -jnp.maximum(z, 0.0) - sp, 0.0)
            hi = l1.astype(BF16)
            lo = (l1 - hi.astype(F32)).astype(BF16)
            e = jnp.dot(tri2, jnp.concatenate([hi, lo], axis=0), preferred_element_type=F32)
            a = a_ref[h]
            w = jnp.where(mask, jnp.exp(log_beta + a + e), 0.0).astype(BF16)
            acc_ref[h] += jnp.dot(vt_ref[0, h, :, pl.ds(k0, tk)], w, preferred_element_type=F32)
            a_new = a + jnp.sum(l1, axis=0, keepdims=True)
            a_ref[h] = a_new
            amax = a_new if amax is None else jnp.maximum(amax, a_new)
        return j - 1, jnp.max(amax) > EXP_UNDERFLOW

    j0 = ((i + 1) * tq - 1) // tk
    lax.while_loop(lambda c: jnp.logical_and(c[0] >= 0, c[1]), body, (j0, True))
    o_ref[0] = acc_ref[...].astype(o_ref.dtype)


def _sb_attention(q, k, vt, tq=128, tk=128):
    b, h, s, dh = q.shape
    return pl.pallas_call(
        functools.partial(_sb_kernel, tq=tq, tk=tk),
        grid=(b, s // tq),
        in_specs=[pl.BlockSpec((1, h, tq, dh), lambda bi, i: (bi, 0, i, 0)),
                  pl.BlockSpec((1, h, s, dh), lambda bi, i: (bi, 0, 0, 0)),
                  pl.BlockSpec((1, h, dh, s), lambda bi, i: (bi, 0, 0, 0))],
        out_specs=pl.BlockSpec((1, h, dh, tq), lambda bi, i: (bi, 0, 0, i)),
        out_shape=jax.ShapeDtypeStruct((b, h, dh, s), BF16),
        scratch_shapes=[pltpu.VMEM((h, dh, tq), F32), pltpu.VMEM((h, 1, tq), F32)],
        compiler_params=_params("parallel", "parallel"),
        name="sb_attention",
    )(q, k, vt)


def _ssm_kmat_kernel(l_ref, b_ref, o_ref):
    for g in range(l_ref.shape[0]):
        o_ref[g] = jnp.dot(l_ref[g], b_ref[g], precision=lax.Precision.HIGHEST,
                           preferred_element_type=F32)


def _ssm_kmat(lcat, bcat, gb=8):
    g, r, n2 = lcat.shape
    c = bcat.shape[2]
    return pl.pallas_call(
        _ssm_kmat_kernel,
        grid=(g // gb,),
        in_specs=[pl.BlockSpec((gb, r, n2), lambda i: (i, 0, 0)),
                  pl.BlockSpec((gb, n2, c), lambda i: (i, 0, 0))],
        out_specs=pl.BlockSpec((gb, r, c), lambda i: (i, 0, 0)),
        out_shape=jax.ShapeDtypeStruct((g, r, c), F32),
        compiler_params=_params("parallel"),
        name="ssm_kmat",
    )(lcat, bcat)


def _cmul(ar, ai, xr, xi):
    return ar * xr - ai * xi, ar * xi + ai * xr


def _ssm_kernel(u_ref, pt_ref, qr_ref, qi_ref, rr_ref, ri_ref, al_ref, aseg_ref, d_ref, y_ref,
                xr_ref, xi_ref, *, gb, nch, nseg, batch):
    rows = nch * SUBLANES
    for g in range(gb):
        ub = u_ref[g]
        xr_ref[g] = jnp.dot(ub, qr_ref[g], preferred_element_type=F32).reshape(nch, SUBLANES, SSM_STATE)
        xi_ref[g] = jnp.dot(ub, qi_ref[g], preferred_element_type=F32).reshape(nch, SUBLANES, SSM_STATE)

    def scan_body(c, carry):
        new = []
        for g in range(gb):
            sr, si = carry[g]
            inr = xr_ref[g, c]
            ini = xi_ref[g, c]
            xr_ref[g, c] = sr
            xi_ref[g, c] = si
            pr, pi = _cmul(al_ref[g, 0:1, :], al_ref[g, 1:2, :], sr, si)
            new.append((pr + inr, pi + ini))
        return tuple(new)

    zero = jnp.zeros((SUBLANES, SSM_STATE), F32)
    ends = lax.fori_loop(0, nch, scan_body, tuple((zero, zero) for _ in range(gb)))

    if nseg > 1:
        row = lax.broadcasted_iota(I32, (SUBLANES, SSM_STATE), 0)
        cins = []
        for g in range(gb):
            er, ei = ends[g]
            cr, ci = zero, zero
            for _ in range(nseg - 1):
                pr, pi = _cmul(aseg_ref[g, 0:1, :], aseg_ref[g, 1:2, :], cr, ci)
                cr = jnp.where(row >= batch, pltpu.roll(er + pr, batch, 0), 0.0)
                ci = jnp.where(row >= batch, pltpu.roll(ei + pi, batch, 0), 0.0)
            cins.append((cr, ci))

        def corr_body(c, carry):
            new = []
            for g in range(gb):
                cr, ci = carry[g]
                xr_ref[g, c] += cr
                xi_ref[g, c] += ci
                new.append(_cmul(al_ref[g, 0:1, :], al_ref[g, 1:2, :], cr, ci))
            return tuple(new)

        lax.fori_loop(0, nch, corr_body, tuple(cins))

    for g in range(gb):
        ub = u_ref[g]
        xcr = xr_ref[g].reshape(rows, SSM_STATE).astype(BF16)
        xci = xi_ref[g].reshape(rows, SSM_STATE).astype(BF16)
        y = jnp.dot(ub, pt_ref[g], preferred_element_type=F32)
        y += jnp.dot(xcr, rr_ref[g], preferred_element_type=F32)
        y += jnp.dot(xci, ri_ref[g], preferred_element_type=F32)
        y_ref[g] = y + ub.astype(F32) * d_ref[g]


def _ssm_scan(u_g, pt, qr, qi, rr, ri, al, aseg, d_t, *, nch, nseg, batch, gb=4):
    g, rows, w = u_g.shape
    n = SSM_STATE
    full3 = lambda a, b: pl.BlockSpec((gb, a, b), lambda i: (i, 0, 0))
    return pl.pallas_call(
        functools.partial(_ssm_kernel, gb=gb, nch=nch, nseg=nseg, batch=batch),
        grid=(g // gb,),
        in_specs=[full3(rows, w), full3(w, w), full3(w, n), full3(w, n), full3(n, w), full3(n, w),
                  full3(2, n), full3(2, n), full3(1, w)],
        out_specs=full3(rows, w),
        out_shape=jax.ShapeDtypeStruct((g, rows, w), F32),
        scratch_shapes=[pltpu.VMEM((gb, nch, SUBLANES, n), F32), pltpu.VMEM((gb, nch, SUBLANES, n), F32)],
        compiler_params=_params("parallel"),
        name="ssm_scan",
    )(u_g, pt, qr, qi, rr, ri, al, aseg, d_t)


def _ssm(u, log_dt, lam_re, lam_im, b_re, b_im, c_re, c_im, d):
    bsz, s, _ = u.shape
    L, G, C, N = SSM_CHUNK, SSM_GROUPS, SSM_GROUP, SSM_STATE
    nseg = SUBLANES // bsz
    nch = s // (L * nseg)
    dt = jnp.exp(log_dt.astype(F32))[:, None]
    lr = lam_re.astype(F32)
    li = lam_im.astype(F32)

    def apow(tau):
        tau = jnp.asarray(tau, F32)[..., None, None]
        mag = jnp.exp(tau * (lr * dt))
        ang = tau * (li * dt)
        return mag * jnp.cos(ang), mag * jnp.sin(ang)

    pr, pi = apow(np.arange(L + 1))
    den = lr * lr + li * li
    nr = pr[1] - 1.0
    coef_r = ((nr * lr + pi[1] * li) / den)[..., None]
    coef_i = ((pi[1] * lr - nr * li) / den)[..., None]
    br = b_re.astype(F32)
    bi = b_im.astype(F32)
    bbar_r = coef_r * br - coef_i * bi
    bbar_i = coef_r * bi + coef_i * br
    cr = c_re.astype(F32)
    ci = c_im.astype(F32)
    car = cr[None] * pr[:, :, None, :] - ci[None] * pi[:, :, None, :]
    cai = cr[None] * pi[:, :, None, :] + ci[None] * pr[:, :, None, :]
    lcat = jnp.concatenate([car[:L], -cai[:L]], axis=-1)
    lcat = lcat.transpose(1, 0, 2, 3).reshape(G, L * C, 2 * N)
    bcat = jnp.concatenate([bbar_r, bbar_i], axis=1)
    kmat = _ssm_kmat(lcat, bcat).reshape(G, L, C, C)
    tt = np.arange(L)[:, None]
    ss = np.arange(L)[None, :]
    kg = kmat[:, np.clip(tt - ss, 0, None)]
    kg = jnp.where((tt >= ss)[None, :, :, None, None], kg, 0.0)
    pt = kg.transpose(0, 2, 4, 1, 3).reshape(G, L * C, L * C).astype(BF16)
    prq = pr[L - 1 - np.arange(L)]
    piq = pi[L - 1 - np.arange(L)]
    qr = prq[..., None] * bbar_r[None] - piq[..., None] * bbar_i[None]
    qi = prq[..., None] * bbar_i[None] + piq[..., None] * bbar_r[None]
    qr = qr.transpose(1, 0, 3, 2).reshape(G, L * C, N).astype(BF16)
    qi = qi.transpose(1, 0, 3, 2).reshape(G, L * C, N).astype(BF16)
    rr = car[1:].transpose(1, 3, 0, 2).reshape(G, N, L * C).astype(BF16)
    ri = (-cai[1:]).transpose(1, 3, 0, 2).reshape(G, N, L * C).astype(BF16)
    al = jnp.stack([pr[L], pi[L]], axis=1)
    sr, si = apow(np.asarray([L * nch]))
    aseg = jnp.stack([sr[0], si[0]], axis=1)
    d_t = jnp.tile(d.astype(F32), (1, L))[:, None, :]

    u_g = u.reshape(bsz, nseg, nch, L, G, C).transpose(4, 2, 1, 0, 3, 5).reshape(G, nch * nseg * bsz, L * C)
    y_g = _ssm_scan(u_g, pt, qr, qi, rr, ri, al, aseg, d_t, nch=nch, nseg=nseg, batch=bsz)
    y = y_g.reshape(G, nch, nseg, bsz, L, C).transpose(3, 2, 1, 4, 0, 5)
    return y.reshape(bsz * s, G * C)


def _mix_even_kernel(a_ref, y_ref, x_ref, wglu_ref, bglu_ref, wo_ref, g_ref, b_ref, o_ref):
    y = jax.nn.gelu(y_ref[...], approximate=True)
    gate = jnp.dot(y.astype(BF16), wglu_ref[...], preferred_element_type=F32) + bglu_ref[...]
    y = y * (1.0 / (1.0 + jnp.exp(-gate)))
    h = jnp.dot(a_ref[...], wo_ref[:SB_WIDTH, :], preferred_element_type=F32)
    h += jnp.dot(y.astype(BF16), wo_ref[SB_WIDTH:, :], preferred_element_type=F32)
    o_ref[...] = _layer_norm(ALPHA * x_ref[...] + h, g_ref[...], b_ref[...])


def _mix_even(a_out, y, x2d, w_glu, b_glu, w_out, g, b, tm=512):
    m, d = x2d.shape
    row = lambda w: pl.BlockSpec((tm, w), lambda i: (i, 0))
    const = lambda r, c: pl.BlockSpec((r, c), lambda i: (0, 0))
    return pl.pallas_call(
        _mix_even_kernel,
        grid=(m // tm,),
        in_specs=[row(SB_WIDTH), row(SSM_WIDTH), row(d), const(SSM_WIDTH, SSM_WIDTH), const(1, SSM_WIDTH),
                  const(d, d), const(1, d), const(1, d)],
        out_specs=row(d),
        out_shape=jax.ShapeDtypeStruct((m, d), F32),
        compiler_params=_params("parallel"),
        name="mix_even",
    )(a_out, y, x2d, w_glu, b_glu, w_out, g, b)


def _mix_odd_kernel(o_in_ref, x_ref, wo_ref, g_ref, b_ref, o_ref):
    h = jnp.dot(o_in_ref[...], wo_ref[...], preferred_element_type=F32)
    o_ref[...] = _layer_norm(ALPHA * x_ref[...] + h, g_ref[...], b_ref[...])


def _mix_odd(o_in, x2d, w_out, g, b, tm=512):
    m, d = x2d.shape
    row = lambda w: pl.BlockSpec((tm, w), lambda i: (i, 0))
    const = lambda r, c: pl.BlockSpec((r, c), lambda i: (0, 0))
    return pl.pallas_call(
        _mix_odd_kernel,
        grid=(m // tm,),
        in_specs=[row(d), row(d), const(d, d), const(1, d), const(1, d)],
        out_specs=row(d),
        out_shape=jax.ShapeDtypeStruct((m, d), F32),
        compiler_params=_params("parallel"),
        name="mix_odd",
    )(o_in, x2d, w_out, g, b)


def _mlp_kernel(x_ref, w1_ref, w2_ref, g_ref, b_ref, o_ref, *, fchunk):
    x = x_ref[...]
    xb = x.astype(BF16)
    acc = jnp.zeros(x.shape, F32)
    for f in range(0, w1_ref.shape[1], fchunk):
        h = jnp.dot(xb, w1_ref[:, f:f + fchunk], preferred_element_type=F32)
        h = jnp.maximum(h, 0.0)
        acc += jnp.dot((h * h).astype(BF16), w2_ref[f:f + fchunk, :], preferred_element_type=F32)
    o_ref[...] = _layer_norm(ALPHA * x + acc, g_ref[...], b_ref[...])


def _mlp(x2d, w1, w2, g, b, tm=512, fchunk=1024):
    m, d = x2d.shape
    f = w1.shape[1]
    row = pl.BlockSpec((tm, d), lambda i: (i, 0))
    once = lambda r, c: pl.BlockSpec((r, c), lambda i: (0, 0), pipeline_mode=pl.Buffered(1))
    return pl.pallas_call(
        functools.partial(_mlp_kernel, fchunk=fchunk),
        grid=(m // tm,),
        in_specs=[row, once(d, f), once(f, d), once(1, d), once(1, d)],
        out_specs=row,
        out_shape=jax.ShapeDtypeStruct((m, d), F32),
        compiler_params=_params("parallel"),
        name="mlp",
    )(x2d, w1, w2, g, b)


def _bf16_split3(x):
    x = np.asarray(x, np.float32)
    hi = x.astype(BF16).astype(np.float32)
    mid = (x - hi).astype(BF16).astype(np.float32)
    lo = (x - hi - mid).astype(BF16).astype(np.float32)
    return hi, mid, lo


def _alibi_columns(s):
    slopes = 2.0 ** (-8.0 * np.arange(1, DSA_HEADS + 1) / DSA_HEADS)
    parts = _bf16_split3(slopes)
    assert np.all(parts[0] + parts[1] + parts[2] == slopes.astype(np.float32))
    q_cols = np.stack(parts * 2, axis=-1) / QK_SCALE
    pos = np.arange(s)
    coarse = (pos // 64 * 64).astype(np.float32)
    fine = (pos % 64).astype(np.float32)
    k_cols = np.stack([coarse] * 3 + [fine] * 3, axis=-1)
    return jnp.asarray(q_cols, BF16), jnp.asarray(k_cols, BF16)


def _key_to_float(key):
    bits = key ^ ((key >> 31) & jnp.int32(0x7FFFFFFF))
    return lax.bitcast_convert_type(bits, F32)


def _dsa_kernel(q_ref, k_ref, vt_ref, qi_ref, ki_ref, wit_ref, o_ref, sc_ref, m_ref, acc_ref, qs_ref,
                bias_ref, xs_ref,
                *, tq, tkb, tk, topk):
    i = pl.program_id(1)
    t0 = i * tq
    n_idx_heads = qi_ref.shape[1]
    kv_heads = k_ref.shape[1]
    rep = q_ref.shape[2]
    dh = o_ref.shape[2]

    nsb = (t0 + tq + tkb - 1) // tkb
    srow = lax.broadcasted_iota(I32, (tkb, tq), 0)
    tcol = t0 + lax.broadcasted_iota(I32, (tkb, tq), 1)
    wt = wit_ref[0] * IDX_SCALE

    def score_body(jb, carry):
        s0 = pl.multiple_of(jb * tkb, tkb)
        kib = ki_ref[0, pl.ds(s0, tkb), :]
        sc = jnp.zeros((tkb, tq), F32)
        for h in range(n_idx_heads):
            z = lax.dot_general(kib, qi_ref[0, h], _NT, preferred_element_type=F32)
            sc += wt[h:h + 1, :] * jnp.maximum(z, 0.0)
        sc_ref[pl.ds(s0, tkb), :] = jnp.where(s0 + srow <= tcol, sc, -jnp.inf)
        return carry

    lax.fori_loop(0, nsb, score_body, 0)

    def count(pred):
        def body(jb, acc):
            s0 = pl.multiple_of(jb * tkb, tkb)
            ind = jnp.where(pred(sc_ref[pl.ds(s0, tkb), :], s0 + srow), 1.0, 0.0)
            return acc + jnp.sum(ind.reshape(tkb // COUNT_ROWS, COUNT_ROWS, tq), axis=0)
        acc = lax.fori_loop(0, nsb, body, jnp.zeros((COUNT_ROWS, tq), F32))
        return jnp.sum(acc, axis=0, keepdims=True)

    kf = float(topk)
    c0 = count(lambda sc, sidx: sc >= 0.0)
    ok0 = c0 >= kf
    thr_key0 = jnp.where(ok0, jnp.int32(0), jnp.int32(INT_MIN))
    cnt0 = jnp.where(ok0, c0, 0.0)

    def thr_cond(carry):
        it, _, cnt = carry
        return jnp.logical_and(it < 31, jnp.max(jnp.abs(cnt - kf)) > 0.0)

    def thr_body(carry):
        it, thr_key, cnt = carry
        cand = thr_key | (jnp.int32(1) << (30 - it))
        cand_f = _key_to_float(cand)
        c = count(lambda sc, sidx: sc >= cand_f)
        ok = c >= kf
        return it + 1, jnp.where(ok, cand, thr_key), jnp.where(ok, c, cnt)

    _, thr_key, _ = lax.while_loop(thr_cond, thr_body, (jnp.int32(0), thr_key0, cnt0))
    thr = jnp.where(thr_key == INT_MIN, -jnp.inf, _key_to_float(thr_key))
    n_gt = count(lambda sc, sidx: sc > thr)
    n_ge = count(lambda sc, sidx: sc >= thr)
    need = kf - n_gt
    has_ties = jnp.max(n_ge) > kf

    def cut_body(it, cut):
        cand = cut | (jnp.int32(1) << (12 - it))
        c = count(lambda sc, sidx: jnp.logical_and(sc == thr, sidx < cand))
        return jnp.where(c <= need, cand, cut)

    cut = lax.fori_loop(0, jnp.where(has_ties, 13, 0), cut_body, jnp.zeros((1, tq), I32))
    cut = jnp.where(has_ties, cut, jnp.int32(2 ** 30))

    m_ref[...] = jnp.full(m_ref.shape, NEG_BIG, F32)
    acc_ref[...] = jnp.zeros_like(acc_ref)
    for g in range(kv_heads):
        for r in range(rep):
            qs_ref[g * rep + r] = q_ref[0, g, r] * QK_SCALE
    nkb = (t0 + tq + tk - 1) // tk
    srow_k = lax.broadcasted_iota(I32, (tk, tq), 0)
    tcol_k = t0 + lax.broadcasted_iota(I32, (tk, tq), 1)

    def attn_body(jb, carry):
        s0 = pl.multiple_of(jb * tk, tk)
        sc = sc_ref[pl.ds(s0, tk), :]
        sidx = s0 + srow_k
        sel = jnp.logical_or(sc > thr, jnp.logical_and(sc == thr, sidx < cut))
        sel = jnp.logical_and(sel, sidx <= tcol_k)
        bias_ref[...] = jnp.where(sel, 0.0, NEG_BIG)
        maxima = []
        for g in range(kv_heads):
            kb = k_ref[0, g, pl.ds(s0, tk), :]
            for r in range(rep):
                h = g * rep + r
                x = lax.dot_general(kb, qs_ref[h], _NT, preferred_element_type=F32) + bias_ref[...]
                xs_ref[h] = x
                maxima.append(jnp.max(x, axis=0, keepdims=True))
        for g in range(kv_heads):
            vt = vt_ref[0, g, :, pl.ds(s0, tk)]
            for r in range(rep):
                h = g * rep + r
                m_prev = m_ref[h]
                m_new = jnp.maximum(m_prev, maxima[h])
                p = jnp.exp(xs_ref[h] - m_new).astype(BF16)
                acc_ref[h] = jnp.exp(m_prev - m_new) * acc_ref[h] + jnp.dot(
                    vt, p, preferred_element_type=F32)
                m_ref[h] = m_new
        return carry

    lax.fori_loop(0, nkb, attn_body, 0)
    for h in range(kv_heads * rep):
        acc = acc_ref[h]
        o_ref[0, h] = (acc[:dh] / acc[dh:dh + 1]).astype(o_ref.dtype)


def _dsa_attention(q, k, vt, qi, ki, wit, topk, tq=256, tkb=512, tk=256):
    b, kv, rep, s, e = q.shape
    dh2 = vt.shape[2]
    ih, di = qi.shape[1], qi.shape[3]
    tkb = min(tkb, s)
    tk = min(tk, s)
    return pl.pallas_call(
        functools.partial(_dsa_kernel, tq=tq, tkb=tkb, tk=tk, topk=topk),
        grid=(b, s // tq),
        in_specs=[pl.BlockSpec((1, kv, rep, tq, e), lambda bi, i: (bi, 0, 0, i, 0)),
                  pl.BlockSpec((1, kv, s, e), lambda bi, i: (bi, 0, 0, 0)),
                  pl.BlockSpec((1, kv, dh2, s), lambda bi, i: (bi, 0, 0, 0)),
                  pl.BlockSpec((1, ih, tq, di), lambda bi, i: (bi, 0, i, 0)),
                  pl.BlockSpec((1, s, di), lambda bi, i: (bi, 0, 0)),
                  pl.BlockSpec((1, ih, tq), lambda bi, i: (bi, 0, i))],
        out_specs=pl.BlockSpec((1, kv * rep, HEAD_DIM, tq), lambda bi, i: (bi, 0, 0, i)),
        out_shape=jax.ShapeDtypeStruct((b, kv * rep, HEAD_DIM, s), BF16),
        scratch_shapes=[pltpu.VMEM((s, tq), F32),
                        pltpu.VMEM((kv * rep, 1, tq), F32),
                        pltpu.VMEM((kv * rep, dh2, tq), F32),
                        pltpu.VMEM((kv * rep, tq, e), BF16),
                        pltpu.VMEM((tk, tq), F32),
                        pltpu.VMEM((kv * rep, tk, tq), F32)],
        compiler_params=_params("parallel", "parallel"),
        name="dsa_attention",
    )(q, k, vt, qi, ki, wit)


def _even_layer(x2d, bsz, s, w_in, ssm_params, w_glu, b_glu, w_out, g_mix, b_mix):
    proj = _proj(x2d, w_in.astype(BF16)).reshape(bsz, s, -1)
    heads = lambda a: a.reshape(bsz, s, SB_HEADS, HEAD_DIM)
    q = heads(proj[..., :SB_WIDTH]).transpose(0, 2, 1, 3)
    k = heads(proj[..., SB_WIDTH:2 * SB_WIDTH]).transpose(0, 2, 1, 3)
    vt = heads(proj[..., 2 * SB_WIDTH:3 * SB_WIDTH]).transpose(0, 2, 3, 1)
    u = proj[..., 3 * SB_WIDTH:]
    a_t = _sb_attention(q, k, vt)
    a_out = a_t.transpose(0, 3, 1, 2).reshape(bsz * s, SB_WIDTH)
    y = _ssm(u, *ssm_params)
    return _mix_even(a_out, y, x2d, w_glu.astype(BF16), b_glu.astype(F32)[None], w_out.astype(BF16),
                     g_mix.astype(F32)[None], b_mix.astype(F32)[None])


def _odd_layer(x2d, bsz, s, w_in, w_out, g_mix, b_mix):
    proj = _proj(x2d, w_in.astype(BF16)).reshape(bsz, s, -1)
    kvw = DSA_KV_HEADS * HEAD_DIM
    o0 = DSA_HEADS * HEAD_DIM
    q_cols, k_cols = _alibi_columns(s)
    n_extra = q_cols.shape[1]
    pad = LANES - HEAD_DIM - n_extra
    q = proj[..., :o0].reshape(bsz, s, DSA_HEADS, HEAD_DIM)
    q = jnp.concatenate([q, jnp.broadcast_to(q_cols, (bsz, s, DSA_HEADS, n_extra)),
                         jnp.zeros((bsz, s, DSA_HEADS, pad), BF16)], axis=-1)
    q = q.reshape(bsz, s, DSA_KV_HEADS, DSA_REP, LANES).transpose(0, 2, 3, 1, 4)
    k = proj[..., o0:o0 + kvw].reshape(bsz, s, DSA_KV_HEADS, HEAD_DIM)
    k = jnp.concatenate([k, jnp.broadcast_to(k_cols[None, :, None, :], (bsz, s, DSA_KV_HEADS, n_extra)),
                         jnp.zeros((bsz, s, DSA_KV_HEADS, pad), BF16)], axis=-1).transpose(0, 2, 1, 3)
    v = proj[..., o0 + kvw:o0 + 2 * kvw].reshape(bsz, s, DSA_KV_HEADS, HEAD_DIM)
    v = jnp.concatenate([v, jnp.ones((bsz, s, DSA_KV_HEADS, 1), BF16),
                         jnp.zeros((bsz, s, DSA_KV_HEADS, BF16_SUBLANES - 1), BF16)], axis=-1)
    vt = v.transpose(0, 2, 3, 1)
    o1 = o0 + 2 * kvw
    qi = proj[..., o1:o1 + IDX_HEADS * IDX_DIM].reshape(bsz, s, IDX_HEADS, IDX_DIM).transpose(0, 2, 1, 3)
    o2 = o1 + IDX_HEADS * IDX_DIM
    ki = proj[..., o2:o2 + IDX_DIM]
    wit = proj[..., o2 + IDX_DIM:o2 + IDX_DIM + IDX_HEADS].astype(F32).transpose(0, 2, 1)
    o = _dsa_attention(q, k, vt, qi, ki, wit, min(TOPK_MAX, s // 4))
    o = o.transpose(0, 3, 1, 2).reshape(bsz * s, DSA_HEADS * HEAD_DIM)
    return _mix_odd(o, x2d, w_out.astype(BF16), g_mix.astype(F32)[None], b_mix.astype(F32)[None])


def kernel(x, sb_ssm_w_in, ssm_log_dt, ssm_lam_re, ssm_lam_im, ssm_b_re, ssm_b_im, ssm_c_re, ssm_c_im, ssm_d,
           ssm_w_glu, ssm_b_glu, sb_ssm_w_out, dsa_w_in, dsa_w_out, ln_mix_g, ln_mix_b, ln_ffn_g, ln_ffn_b,
           mlp_w1, mlp_w2):
    bsz, s, d = x.shape
    x2d = x.reshape(bsz * s, d).astype(F32)
    depth = mlp_w1.shape[0]
    for i in range(depth):
        j = i // 2
        if i % 2 == 0:
            ssm_params = (ssm_log_dt[j], ssm_lam_re[j], ssm_lam_im[j], ssm_b_re[j], ssm_b_im[j],
                          ssm_c_re[j], ssm_c_im[j], ssm_d[j])
            x2d = _even_layer(x2d, bsz, s, sb_ssm_w_in[j], ssm_params, ssm_w_glu[j], ssm_b_glu[j],
                              sb_ssm_w_out[j], ln_mix_g[i], ln_mix_b[i])
        else:
            n_in = dsa_w_in.shape[2]
            pad = (-n_in) % LANES
            w_in = jnp.pad(dsa_w_in[j], ((0, 0), (0, pad)))
            x2d = _odd_layer(x2d, bsz, s, w_in, dsa_w_out[j], ln_mix_g[i], ln_mix_b[i])
        x2d = _mlp(x2d, mlp_w1[i].astype(BF16), mlp_w2[i].astype(BF16),
                   ln_ffn_g[i].astype(F32)[None], ln_ffn_b[i].astype(F32)[None])
    return x2d.reshape(bsz, s, d).astype(x.dtype)
```

```python
import functools
import math

import jax
import jax.numpy as jnp
import numpy as np
from jax import lax
from jax.experimental import pallas as pl
from jax.experimental.pallas import tpu as pltpu

F32 = jnp.float32
BF16 = jnp.bfloat16
I32 = jnp.int32

D_MODEL = 1024
HEAD_DIM = 64
SB_WIDTH = 512
SB_HEADS = SB_WIDTH // HEAD_DIM
SSM_WIDTH = 512
SSM_GROUP = 16
SSM_GROUPS = SSM_WIDTH // SSM_GROUP
SSM_STATE = 64
DSA_HEADS = 16
DSA_KV_HEADS = 4
DSA_REP = DSA_HEADS // DSA_KV_HEADS
IDX_HEADS = 8
IDX_DIM = 32
TOPK_MAX = 256
D_FF = 4 * D_MODEL
DEPTH = 4
ALPHA = (2 * DEPTH) ** 0.25
LN_EPS = 1e-5
IDX_SCALE = (IDX_DIM ** -0.5) * (IDX_HEADS ** -0.5)
QK_SCALE = HEAD_DIM ** -0.5

SSM_CHUNK = 16
SUBLANES = 8
BF16_SUBLANES = 16
LANES = 128
VMEM_LIMIT = 56 * 1024 * 1024
EXP_UNDERFLOW = -104.0
NEG_BIG = -1e30
INT_MIN = -2 ** 31
COUNT_ROWS = 64
TR_CHUNK = 256

_NT = (((1,), (1,)), ((), ()))


def _params(*sem):
    return pltpu.CompilerParams(dimension_semantics=sem, vmem_limit_bytes=VMEM_LIMIT)


def _layer_norm(r, g, b):
    mu = jnp.mean(r, axis=-1, keepdims=True)
    c = r - mu
    var = jnp.mean(c * c, axis=-1, keepdims=True)
    return c * lax.rsqrt(var + LN_EPS) * g + b


def _proj_kernel(x_ref, w_ref, o_ref):
    o_ref[...] = jnp.dot(x_ref[...].astype(BF16), w_ref[...],
                         preferred_element_type=F32).astype(o_ref.dtype)


def _proj(x2d, w, tm=512):
    m, k = x2d.shape
    n = w.shape[1]
    return pl.pallas_call(
        _proj_kernel,
        grid=(m // tm,),
        in_specs=[pl.BlockSpec((tm, k), lambda i: (i, 0)),
                  pl.BlockSpec((k, n), lambda i: (0, 0))],
        out_specs=pl.BlockSpec((tm, n), lambda i: (i, 0)),
        out_shape=jax.ShapeDtypeStruct((m, n), BF16),
        compiler_params=_params("parallel"),
        name="proj",
    )(x2d, w)


def _sb_kernel(q_ref, k_ref, v_ref, o_ref, vt_ref, qm_ref, acc_ref, a_ref, *, tq, tk):
    i = pl.program_id(1)
    s = k_ref.shape[1]
    dh = HEAD_DIM
    heads = q_ref.shape[2] // dh
    pairs = heads // 2

    @pl.when(i == 0)
    def _():
        def transpose_chunk(c, carry):
            c0 = pl.multiple_of(c * TR_CHUNK, TR_CHUNK)
            vt_ref[:, pl.ds(c0, TR_CHUNK)] = v_ref[0, pl.ds(c0, TR_CHUNK), :].astype(F32).T.astype(BF16)
            return carry
        lax.fori_loop(0, s // TR_CHUNK, transpose_chunk, 0)

    lane = lax.broadcasted_iota(I32, (tq, LANES), 1)
    for p in range(pairs):
        qp = q_ref[0, :, p * LANES:(p + 1) * LANES].astype(F32) * QK_SCALE
        qm_ref[2 * p] = jnp.where(lane < dh, qp, 0.0).astype(BF16)
        qm_ref[2 * p + 1] = jnp.where(lane >= dh, qp, 0.0).astype(BF16)
    acc_ref[...] = jnp.zeros_like(acc_ref)
    a_ref[...] = jnp.zeros_like(a_ref)
    rows = lax.broadcasted_iota(I32, (tk, tq), 0)
    tcol = i * tq + lax.broadcasted_iota(I32, (tk, tq), 1)
    tri = lax.broadcasted_iota(I32, (tk, tk), 1) > lax.broadcasted_iota(I32, (tk, tk), 0)
    tri = jnp.where(tri, 1.0, 0.0).astype(BF16)
    tri2 = jnp.concatenate([tri, tri], axis=1)

    def body(carry):
        j, _ = carry
        k0 = pl.multiple_of(j * tk, tk)
        mask = (k0 + rows) < tcol
        hs = range(heads)
        kbs = [k_ref[0, pl.ds(k0, tk), p * LANES:(p + 1) * LANES] for p in range(pairs)]
        zs = [lax.dot_general(kbs[h // 2], qm_ref[h], _NT, preferred_element_type=F32) for h in hs]
        sps = [jnp.log(1.0 + jnp.exp(-jnp.abs(z))) for z in zs]
        log_betas = [jnp.minimum(z, 0.0) - sp for z, sp in zip(zs, sps)]
        l1s = [jnp.where(mask, -jnp.maximum(z, 0.0) - sp, 0.0) for z, sp in zip(zs, sps)]
        his = [l1.astype(BF16) for l1 in l1s]
        los = [(l1 - hi.astype(F32)).astype(BF16) for l1, hi in zip(l1s, his)]
        es = [jnp.dot(tri2, jnp.concatenate([hi, lo], axis=0), preferred_element_type=F32)
              for hi, lo in zip(his, los)]
        a_olds = [a_ref[h] for h in hs]
        ws = [jnp.where(mask, jnp.exp(lb + a + e), 0.0).astype(BF16) for lb, a, e in zip(log_betas, a_olds, es)]
        amax = None
        for h in hs:
            acc_ref[h] += jnp.dot(vt_ref[h * dh:(h + 1) * dh, pl.ds(k0, tk)], ws[h], preferred_element_type=F32)
            a_new = a_olds[h] + jnp.sum(l1s[h], axis=0, keepdims=True)
            a_ref[h] = a_new
            amax = a_new if amax is None else jnp.maximum(amax, a_new)
        return j - 1, jnp.max(amax) > EXP_UNDERFLOW

    j0 = ((i + 1) * tq - 1) // tk
    lax.while_loop(lambda c: jnp.logical_and(c[0] >= 0, c[1]), body, (j0, True))
    for p in range(pairs):
        both = jnp.concatenate([acc_ref[2 * p], acc_ref[2 * p + 1]], axis=0)
        o_ref[0, :, p * LANES:(p + 1) * LANES] = both.T.astype(o_ref.dtype)


def _sb_attention(proj, tq=256, tk=128):
    b, s, _ = proj.shape
    w = SB_WIDTH
    return pl.pallas_call(
        functools.partial(_sb_kernel, tq=tq, tk=tk),
        grid=(b, s // tq),
        in_specs=[pl.BlockSpec((1, tq, w), lambda bi, i: (bi, i, 0)),
                  pl.BlockSpec((1, s, w), lambda bi, i: (bi, 0, 1)),
                  pl.BlockSpec((1, s, w), lambda bi, i: (bi, 0, 2))],
        out_specs=pl.BlockSpec((1, tq, w), lambda bi, i: (bi, i, 0)),
        out_shape=jax.ShapeDtypeStruct((b, s, w), BF16),
        scratch_shapes=[pltpu.VMEM((w, s), BF16),
                        pltpu.VMEM((SB_HEADS, tq, LANES), BF16),
                        pltpu.VMEM((SB_HEADS, HEAD_DIM, tq), F32),
                        pltpu.VMEM((SB_HEADS, 1, tq), F32)],
        compiler_params=_params("parallel", "arbitrary"),
        name="sb_attention",
    )(proj, proj, proj)


def _ssm_kmat_kernel(l_ref, b_ref, o_ref):
    for g in range(l_ref.shape[0]):
        o_ref[g] = jnp.dot(l_ref[g], b_ref[g], precision=lax.Precision.HIGHEST,
                           preferred_element_type=F32)


def _ssm_kmat(lcat, bcat, gb=8):
    g, r, n2 = lcat.shape
    c = bcat.shape[2]
    return pl.pallas_call(
        _ssm_kmat_kernel,
        grid=(g // gb,),
        in_specs=[pl.BlockSpec((gb, r, n2), lambda i: (i, 0, 0)),
                  pl.BlockSpec((gb, n2, c), lambda i: (i, 0, 0))],
        out_specs=pl.BlockSpec((gb, r, c), lambda i: (i, 0, 0)),
        out_shape=jax.ShapeDtypeStruct((g, r, c), F32),
        compiler_params=_params("parallel"),
        name="ssm_kmat",
    )(lcat, bcat)


def _cmul(ar, ai, xr, xi):
    return ar * xr - ai * xi, ar * xi + ai * xr


def _ssm_kernel(u_ref, pt_ref, qr_ref, qi_ref, rr_ref, ri_ref, al_ref, aseg_ref, d_ref, y_ref,
                xr_ref, xi_ref, *, gb, nch, nseg, batch):
    rows = nch * SUBLANES
    for g in range(gb):
        ub = u_ref[g]
        xr_ref[g] = jnp.dot(ub, qr_ref[g], preferred_element_type=F32).reshape(nch, SUBLANES, SSM_STATE)
        xi_ref[g] = jnp.dot(ub, qi_ref[g], preferred_element_type=F32).reshape(nch, SUBLANES, SSM_STATE)

    def scan_body(c, carry):
        new = []
        for g in range(gb):
            sr, si = carry[g]
            inr = xr_ref[g, c]
            ini = xi_ref[g, c]
            xr_ref[g, c] = sr
            xi_ref[g, c] = si
            pr, pi = _cmul(al_ref[g, 0:1, :], al_ref[g, 1:2, :], sr, si)
            new.append((pr + inr, pi + ini))
        return tuple(new)

    zero = jnp.zeros((SUBLANES, SSM_STATE), F32)
    ends = lax.fori_loop(0, nch, scan_body, tuple((zero, zero) for _ in range(gb)))

    if nseg > 1:
        row = lax.broadcasted_iota(I32, (SUBLANES, SSM_STATE), 0)
        cins = []
        for g in range(gb):
            er, ei = ends[g]
            cr, ci = zero, zero
            for _ in range(nseg - 1):
                pr, pi = _cmul(aseg_ref[g, 0:1, :], aseg_ref[g, 1:2, :], cr, ci)
                cr = jnp.where(row >= batch, pltpu.roll(er + pr, batch, 0), 0.0)
                ci = jnp.where(row >= batch, pltpu.roll(ei + pi, batch, 0), 0.0)
            cins.append((cr, ci))

        def corr_body(c, carry):
            new = []
            for g in range(gb):
                cr, ci = carry[g]
                xr_ref[g, c] += cr
                xi_ref[g, c] += ci
                new.append(_cmul(al_ref[g, 0:1, :], al_ref[g, 1:2, :], cr, ci))
            return tuple(new)

        lax.fori_loop(0, nch, corr_body, tuple(cins))

    for g in range(gb):
        ub = u_ref[g]
        xcr = xr_ref[g].reshape(rows, SSM_STATE).astype(BF16)
        xci = xi_ref[g].reshape(rows, SSM_STATE).astype(BF16)
        y = jnp.dot(ub, pt_ref[g], preferred_element_type=F32)
        y += jnp.dot(xcr, rr_ref[g], preferred_element_type=F32)
        y += jnp.dot(xci, ri_ref[g], preferred_element_type=F32)
        y_ref[g] = y + ub.astype(F32) * d_ref[g]


def _ssm_scan(u_g, pt, qr, qi, rr, ri, al, aseg, d_t, *, nch, nseg, batch, gb=4):
    g, rows, w = u_g.shape
    n = SSM_STATE
    full3 = lambda a, b: pl.BlockSpec((gb, a, b), lambda i: (i, 0, 0))
    return pl.pallas_call(
        functools.partial(_ssm_kernel, gb=gb, nch=nch, nseg=nseg, batch=batch),
        grid=(g // gb,),
        in_specs=[full3(rows, w), full3(w, w), full3(w, n), full3(w, n), full3(n, w), full3(n, w),
                  full3(2, n), full3(2, n), full3(1, w)],
        out_specs=full3(rows, w),
        out_shape=jax.ShapeDtypeStruct((g, rows, w), F32),
        scratch_shapes=[pltpu.VMEM((gb, nch, SUBLANES, n), F32), pltpu.VMEM((gb, nch, SUBLANES, n), F32)],
        compiler_params=_params("parallel"),
        name="ssm_scan",
    )(u_g, pt, qr, qi, rr, ri, al, aseg, d_t)


def _ssm(u, log_dt, lam_re, lam_im, b_re, b_im, c_re, c_im, d):
    bsz, s, _ = u.shape
    L, G, C, N = SSM_CHUNK, SSM_GROUPS, SSM_GROUP, SSM_STATE
    nseg = SUBLANES // bsz
    nch = s // (L * nseg)
    dt = jnp.exp(log_dt.astype(F32))[:, None]
    lr = lam_re.astype(F32)
    li = lam_im.astype(F32)

    def apow(tau):
        tau = jnp.asarray(tau, F32)[..., None, None]
        mag = jnp.exp(tau * (lr * dt))
        ang = tau * (li * dt)
        return mag * jnp.cos(ang), mag * jnp.sin(ang)

    pr, pi = apow(np.arange(L + 1))
    den = lr * lr + li * li
    nr = pr[1] - 1.0
    coef_r = ((nr * lr + pi[1] * li) / den)[..., None]
    coef_i = ((pi[1] * lr - nr * li) / den)[..., None]
    br = b_re.astype(F32)
    bi = b_im.astype(F32)
    bbar_r = coef_r * br - coef_i * bi
    bbar_i = coef_r * bi + coef_i * br
    cr = c_re.astype(F32)
    ci = c_im.astype(F32)
    car = cr[None] * pr[:, :, None, :] - ci[None] * pi[:, :, None, :]
    cai = cr[None] * pi[:, :, None, :] + ci[None] * pr[:, :, None, :]
    lcat = jnp.concatenate([car[:L], -cai[:L]], axis=-1)
    lcat = lcat.transpose(1, 0, 2, 3).reshape(G, L * C, 2 * N)
    bcat = jnp.concatenate([bbar_r, bbar_i], axis=1)
    kmat = _ssm_kmat(lcat, bcat).reshape(G, L, C, C)
    tt = np.arange(L)[:, None]
    ss = np.arange(L)[None, :]
    kg = kmat[:, np.clip(tt - ss, 0, None)]
    kg = jnp.where((tt >= ss)[None, :, :, None, None], kg, 0.0)
    pt = kg.transpose(0, 2, 4, 1, 3).reshape(G, L * C, L * C).astype(BF16)
    prq = pr[L - 1 - np.arange(L)]
    piq = pi[L - 1 - np.arange(L)]
    qr = prq[..., None] * bbar_r[None] - piq[..., None] * bbar_i[None]
    qi = prq[..., None] * bbar_i[None] + piq[..., None] * bbar_r[None]
    qr = qr.transpose(1, 0, 3, 2).reshape(G, L * C, N).astype(BF16)
    qi = qi.transpose(1, 0, 3, 2).reshape(G, L * C, N).astype(BF16)
    rr = car[1:].transpose(1, 3, 0, 2).reshape(G, N, L * C).astype(BF16)
    ri = (-cai[1:]).transpose(1, 3, 0, 2).reshape(G, N, L * C).astype(BF16)
    al = jnp.stack([pr[L], pi[L]], axis=1)
    sr, si = apow(np.asarray([L * nch]))
    aseg = jnp.stack([sr[0], si[0]], axis=1)
    d_t = jnp.tile(d.astype(F32), (1, L))[:, None, :]

    u_g = u.reshape(bsz, nseg, nch, L, G, C).transpose(4, 2, 1, 0, 3, 5).reshape(G, nch * nseg * bsz, L * C)
    y_g = _ssm_scan(u_g, pt, qr, qi, rr, ri, al, aseg, d_t, nch=nch, nseg=nseg, batch=bsz)
    y = y_g.reshape(G, nch, nseg, bsz, L, C).transpose(3, 2, 1, 4, 0, 5)
    return y.reshape(bsz * s, G * C)


def _mix_even_kernel(a_ref, y_ref, x_ref, wglu_ref, bglu_ref, wo_ref, g_ref, b_ref, o_ref):
    y = jax.nn.gelu(y_ref[...], approximate=True)
    gate = jnp.dot(y.astype(BF16), wglu_ref[...], preferred_element_type=F32) + bglu_ref[...]
    y = y * (1.0 / (1.0 + jnp.exp(-gate)))
    h = jnp.dot(a_ref[...], wo_ref[:SB_WIDTH, :], preferred_element_type=F32)
    h += jnp.dot(y.astype(BF16), wo_ref[SB_WIDTH:, :], preferred_element_type=F32)
    o_ref[...] = _layer_norm(ALPHA * x_ref[...] + h, g_ref[...], b_ref[...])


def _mix_even(a_out, y, x2d, w_glu, b_glu, w_out, g, b, tm=512):
    m, d = x2d.shape
    row = lambda w: pl.BlockSpec((tm, w), lambda i: (i, 0))
    const = lambda r, c: pl.BlockSpec((r, c), lambda i: (0, 0))
    return pl.pallas_call(
        _mix_even_kernel,
        grid=(m // tm,),
        in_specs=[row(SB_WIDTH), row(SSM_WIDTH), row(d), const(SSM_WIDTH, SSM_WIDTH), const(1, SSM_WIDTH),
                  const(d, d), const(1, d), const(1, d)],
        out_specs=row(d),
        out_shape=jax.ShapeDtypeStruct((m, d), F32),
        compiler_params=_params("parallel"),
        name="mix_even",
    )(a_out, y, x2d, w_glu, b_glu, w_out, g, b)


def _mix_odd_kernel(o_in_ref, x_ref, wo_ref, g_ref, b_ref, o_ref):
    h = jnp.dot(o_in_ref[...], wo_ref[...], preferred_element_type=F32)
    o_ref[...] = _layer_norm(ALPHA * x_ref[...] + h, g_ref[...], b_ref[...])


def _mix_odd(o_in, x2d, w_out, g, b, tm=512):
    m, d = x2d.shape
    row = lambda w: pl.BlockSpec((tm, w), lambda i: (i, 0))
    const = lambda r, c: pl.BlockSpec((r, c), lambda i: (0, 0))
    return pl.pallas_call(
        _mix_odd_kernel,
        grid=(m // tm,),
        in_specs=[row(d), row(d), const(d, d), const(1, d), const(1, d)],
        out_specs=row(d),
        out_shape=jax.ShapeDtypeStruct((m, d), F32),
        compiler_params=_params("parallel"),
        name="mix_odd",
    )(o_in, x2d, w_out, g, b)


def _mlp_kernel(x_ref, w1_ref, w2_ref, g_ref, b_ref, o_ref, *, fchunk):
    x = x_ref[...]
    xb = x.astype(BF16)
    acc = jnp.zeros(x.shape, F32)
    for f in range(0, w1_ref.shape[1], fchunk):
        h = jnp.dot(xb, w1_ref[:, f:f + fchunk], preferred_element_type=F32)
        h = jnp.maximum(h, 0.0)
        acc += jnp.dot((h * h).astype(BF16), w2_ref[f:f + fchunk, :], preferred_element_type=F32)
    o_ref[...] = _layer_norm(ALPHA * x + acc, g_ref[...], b_ref[...])


def _mlp(x2d, w1, w2, g, b, tm=512, fchunk=1024):
    m, d = x2d.shape
    f = w1.shape[1]
    row = pl.BlockSpec((tm, d), lambda i: (i, 0))
    once = lambda r, c: pl.BlockSpec((r, c), lambda i: (0, 0), pipeline_mode=pl.Buffered(1))
    return pl.pallas_call(
        functools.partial(_mlp_kernel, fchunk=fchunk),
        grid=(m // tm,),
        in_specs=[row, once(d, f), once(f, d), once(1, d), once(1, d)],
        out_specs=row,
        out_shape=jax.ShapeDtypeStruct((m, d), F32),
        compiler_params=_params("parallel"),
        name="mlp",
    )(x2d, w1, w2, g, b)


def _bf16_split3(x):
    x = np.asarray(x, np.float32)
    hi = x.astype(BF16).astype(np.float32)
    mid = (x - hi).astype(BF16).astype(np.float32)
    lo = (x - hi - mid).astype(BF16).astype(np.float32)
    return hi, mid, lo


def _alibi_columns(s):
    slopes = 2.0 ** (-8.0 * np.arange(1, DSA_HEADS + 1) / DSA_HEADS)
    parts = _bf16_split3(slopes)
    assert np.all(parts[0] + parts[1] + parts[2] == slopes.astype(np.float32))
    q_cols = np.zeros((DSA_HEADS, 1, HEAD_DIM), np.float32)
    q_cols[:, 0, :6] = np.stack(parts * 2, axis=-1)
    pos = np.arange(s)
    k_cols = np.zeros((s, HEAD_DIM), np.float32)
    k_cols[:, 0:3] = (pos // 64 * 64)[:, None]
    k_cols[:, 3:6] = (pos % 64)[:, None]
    return jnp.asarray(q_cols, BF16), jnp.asarray(k_cols, BF16)


def _key_to_float(key):
    bits = key ^ ((key >> 31) & jnp.int32(0x7FFFFFFF))
    return lax.bitcast_convert_type(bits, F32)


def _dsa_kernel(q_ref, k_ref, v_ref, qi_ref, kiw_ref, kiwq_ref, qcols_ref, kcols_ref, o_ref,
                kext_ref, vt_ref, qie_ref, sc_ref, m_ref, acc_ref, qs_ref, bias_ref, xs_ref,
                *, tq, tkb, tk, topk):
    i = pl.program_id(1)
    t0 = i * tq
    s = k_ref.shape[1]
    dh = HEAD_DIM
    kv_heads = k_ref.shape[2] // dh
    n_heads = q_ref.shape[2] // dh
    rep = n_heads // kv_heads
    n_idx_heads = qi_ref.shape[2] // IDX_DIM

    @pl.when(i == 0)
    def _():
        def prep_chunk(c, carry):
            c0 = pl.multiple_of(c * TR_CHUNK, TR_CHUNK)
            kblk = k_ref[0, pl.ds(c0, TR_CHUNK), :].astype(F32)
            vblk_t = v_ref[0, pl.ds(c0, TR_CHUNK), :].astype(F32).T
            for g in range(kv_heads):
                kext_ref[g, pl.ds(c0, TR_CHUNK), 0:dh] = kblk[:, g * dh:(g + 1) * dh].astype(BF16)
                kext_ref[g, pl.ds(c0, TR_CHUNK), dh:2 * dh] = kcols_ref[pl.ds(c0, TR_CHUNK), :]
                vt_ref[g, 0:dh, pl.ds(c0, TR_CHUNK)] = vblk_t[g * dh:(g + 1) * dh].astype(BF16)
            return carry
        lax.fori_loop(0, s // TR_CHUNK, prep_chunk, 0)
        ones_row = lax.broadcasted_iota(I32, (BF16_SUBLANES, s), 0) == 0
        for g in range(kv_heads):
            vt_ref[g, dh:dh + BF16_SUBLANES, :] = jnp.where(ones_row, 1.0, 0.0).astype(BF16)

    qf = q_ref[0].astype(F32) * QK_SCALE
    for h in range(n_heads):
        qs_ref[h, :, 0:dh] = qf[:, h * dh:(h + 1) * dh].astype(BF16)
        qs_ref[h, :, dh:2 * dh] = jnp.broadcast_to(qcols_ref[h], (tq, dh))
    qif = qi_ref[0].astype(F32)
    lane = lax.broadcasted_iota(I32, (tq, LANES), 1)
    per_tile = LANES // IDX_DIM
    for h in range(n_idx_heads):
        tile = qif[:, (h // per_tile) * LANES:(h // per_tile + 1) * LANES]
        shift = (h % per_tile) * IDX_DIM
        if shift:
            tile = pltpu.roll(tile, LANES - shift, 1)
        qie_ref[h] = jnp.where(lane < IDX_DIM, tile, 0.0).astype(BF16)
    wt = kiwq_ref[0].astype(F32).T[IDX_DIM:IDX_DIM + n_idx_heads] * IDX_SCALE

    nsb = (t0 + tq + tkb - 1) // tkb
    srow = lax.broadcasted_iota(I32, (tkb, tq), 0)
    tcol = t0 + lax.broadcasted_iota(I32, (tkb, tq), 1)

    def score_body(jb, carry):
        s0 = pl.multiple_of(jb * tkb, tkb)
        kib = kiw_ref[0, pl.ds(s0, tkb), :]
        sc = jnp.zeros((tkb, tq), F32)
        for h in range(n_idx_heads):
            z = lax.dot_general(kib, qie_ref[h], _NT, preferred_element_type=F32)
            sc += wt[h:h + 1, :] * jnp.maximum(z, 0.0)
        sc_ref[pl.ds(s0, tkb), :] = jnp.where(s0 + srow <= tcol, sc, -jnp.inf)
        return carry

    lax.fori_loop(0, nsb, score_body, 0)

    def count(pred):
        def body(jb, acc):
            s0 = pl.multiple_of(jb * tkb, tkb)
            ind = jnp.where(pred(sc_ref[pl.ds(s0, tkb), :], s0 + srow), 1.0, 0.0)
            return acc + jnp.sum(ind.reshape(tkb // COUNT_ROWS, COUNT_ROWS, tq), axis=0)
        acc = lax.fori_loop(0, nsb, body, jnp.zeros((COUNT_ROWS, tq), F32))
        return jnp.sum(acc, axis=0, keepdims=True)

    kf = float(topk)
    c0 = count(lambda sc, sidx: sc >= 0.0)
    ok0 = c0 >= kf
    thr_key0 = jnp.where(ok0, jnp.int32(0), jnp.int32(INT_MIN))
    cnt0 = jnp.where(ok0, c0, 0.0)

    def thr_cond(carry):
        it, _, cnt = carry
        return jnp.logical_and(it < 31, jnp.max(jnp.abs(cnt - kf)) > 0.0)

    def thr_body(carry):
        it, thr_key, cnt = carry
        cand = thr_key | (jnp.int32(1) << (30 - it))
        cand_f = _key_to_float(cand)
        c = count(lambda sc, sidx: sc >= cand_f)
        ok = c >= kf
        return it + 1, jnp.where(ok, cand, thr_key), jnp.where(ok, c, cnt)

    _, thr_key, _ = lax.while_loop(thr_cond, thr_body, (jnp.int32(0), thr_key0, cnt0))
    thr = jnp.where(thr_key == INT_MIN, -jnp.inf, _key_to_float(thr_key))
    n_gt = count(lambda sc, sidx: sc > thr)
    n_ge = count(lambda sc, sidx: sc >= thr)
    need = kf - n_gt
    has_ties = jnp.max(n_ge) > kf

    def cut_body(it, cut):
        cand = cut | (jnp.int32(1) << (12 - it))
        c = count(lambda sc, sidx: jnp.logical_and(sc == thr, sidx < cand))
        return jnp.where(c <= need, cand, cut)

    cut = lax.fori_loop(0, jnp.where(has_ties, 13, 0), cut_body, jnp.zeros((1, tq), I32))
    cut = jnp.where(has_ties, cut, jnp.int32(2 ** 30))

    m_ref[...] = jnp.full(m_ref.shape, NEG_BIG, F32)
    acc_ref[...] = jnp.zeros_like(acc_ref)
    nkb = (t0 + tq + tk - 1) // tk
    srow_k = lax.broadcasted_iota(I32, (tk, tq), 0)
    tcol_k = t0 + lax.broadcasted_iota(I32, (tk, tq), 1)

    def attn_body(jb, carry):
        s0 = pl.multiple_of(jb * tk, tk)
        sc = sc_ref[pl.ds(s0, tk), :]
        sidx = s0 + srow_k
        sel = jnp.logical_or(sc > thr, jnp.logical_and(sc == thr, sidx < cut))
        sel = jnp.logical_and(sel, sidx <= tcol_k)
        bias_ref[...] = jnp.where(sel, 0.0, NEG_BIG)
        maxima = []
        for g in range(kv_heads):
            kb = kext_ref[g, pl.ds(s0, tk), :]
            for r in range(rep):
                h = g * rep + r
                x = lax.dot_general(kb, qs_ref[h], _NT, preferred_element_type=F32) + bias_ref[...]
                xs_ref[h] = x
                maxima.append(jnp.max(x, axis=0, keepdims=True))
        for g in range(kv_heads):
            vt = vt_ref[g, :, pl.ds(s0, tk)]
            for r in range(rep):
                h = g * rep + r
                m_prev = m_ref[h]
                m_new = jnp.maximum(m_prev, maxima[h])
                p = jnp.exp(xs_ref[h] - m_new).astype(BF16)
                acc_ref[h] = jnp.exp(m_prev - m_new) * acc_ref[h] + jnp.dot(
                    vt, p, preferred_element_type=F32)
                m_ref[h] = m_new
        return carry

    lax.fori_loop(0, nkb, attn_body, 0)
    for p in range(n_heads // 2):
        outs = []
        for h in (2 * p, 2 * p + 1):
            acc = acc_ref[h]
            outs.append(acc[:dh] / acc[dh:dh + 1])
        o_ref[0, :, p * LANES:(p + 1) * LANES] = jnp.concatenate(outs, axis=0).T.astype(o_ref.dtype)


def _dsa_attention(proj, topk, tq=256, tkb=512, tk=256):
    b, s, n = proj.shape
    dh = HEAD_DIM
    qw = DSA_HEADS * dh
    kvw = DSA_KV_HEADS * dh
    iw = IDX_HEADS * IDX_DIM
    assert qw % kvw == 0 and (qw + 2 * kvw) % iw == 0 and (qw + 2 * kvw + iw) % LANES == 0
    assert IDX_DIM + IDX_HEADS <= LANES and n == qw + 2 * kvw + iw + LANES
    q_cols, k_cols = _alibi_columns(s)
    tkb = min(tkb, s)
    tk = min(tk, s)
    kiw_block = (qw + 2 * kvw + iw) // LANES
    dv = dh + BF16_SUBLANES
    return pl.pallas_call(
        functools.partial(_dsa_kernel, tq=tq, tkb=tkb, tk=tk, topk=topk),
        grid=(b, s // tq),
        in_specs=[pl.BlockSpec((1, tq, qw), lambda bi, i: (bi, i, 0)),
                  pl.BlockSpec((1, s, kvw), lambda bi, i: (bi, 0, qw // kvw)),
                  pl.BlockSpec((1, s, kvw), lambda bi, i: (bi, 0, qw // kvw + 1)),
                  pl.BlockSpec((1, tq, iw), lambda bi, i: (bi, i, (qw + 2 * kvw) // iw)),
                  pl.BlockSpec((1, s, LANES), lambda bi, i: (bi, 0, kiw_block)),
                  pl.BlockSpec((1, tq, LANES), lambda bi, i: (bi, i, kiw_block)),
                  pl.BlockSpec((DSA_HEADS, 1, dh), lambda bi, i: (0, 0, 0)),
                  pl.BlockSpec((s, dh), lambda bi, i: (0, 0))],
        out_specs=pl.BlockSpec((1, tq, qw), lambda bi, i: (bi, i, 0)),
        out_shape=jax.ShapeDtypeStruct((b, s, qw), BF16),
        scratch_shapes=[pltpu.VMEM((DSA_KV_HEADS, s, 2 * dh), BF16),
                        pltpu.VMEM((DSA_KV_HEADS, dv, s), BF16),
                        pltpu.VMEM((IDX_HEADS, tq, LANES), BF16),
                        pltpu.VMEM((s, tq), F32),
                        pltpu.VMEM((DSA_HEADS, 1, tq), F32),
                        pltpu.VMEM((DSA_HEADS, dv, tq), F32),
                        pltpu.VMEM((DSA_HEADS, tq, 2 * dh), BF16),
                        pltpu.VMEM((tk, tq), F32),
                        pltpu.VMEM((DSA_HEADS, tk, tq), F32)],
        compiler_params=_params("parallel", "arbitrary"),
        name="dsa_attention",
    )(proj, proj, proj, proj, proj, proj, q_cols, k_cols)


def _even_layer(x2d, bsz, s, w_in, ssm_params, w_glu, b_glu, w_out, g_mix, b_mix):
    proj = _proj(x2d, w_in.astype(BF16)).reshape(bsz, s, -1)
    a_out = _sb_attention(proj).reshape(bsz * s, SB_WIDTH)
    y = _ssm(proj[..., 3 * SB_WIDTH:], *ssm_params)
    return _mix_even(a_out, y, x2d, w_glu.astype(BF16), b_glu.astype(F32)[None], w_out.astype(BF16),
                     g_mix.astype(F32)[None], b_mix.astype(F32)[None])


def _odd_layer(x2d, bsz, s, w_in, w_out, g_mix, b_mix):
    proj = _proj(x2d, w_in.astype(BF16)).reshape(bsz, s, -1)
    o = _dsa_attention(proj, min(TOPK_MAX, s // 4)).reshape(bsz * s, DSA_HEADS * HEAD_DIM)
    return _mix_odd(o, x2d, w_out.astype(BF16), g_mix.astype(F32)[None], b_mix.astype(F32)[None])


def kernel(x, sb_ssm_w_in, ssm_log_dt, ssm_lam_re, ssm_lam_im, ssm_b_re, ssm_b_im, ssm_c_re, ssm_c_im, ssm_d,
           ssm_w_glu, ssm_b_glu, sb_ssm_w_out, dsa_w_in, dsa_w_out, ln_mix_g, ln_mix_b, ln_ffn_g, ln_ffn_b,
           mlp_w1, mlp_w2):
    bsz, s, d = x.shape
    x2d = x.reshape(bsz * s, d).astype(F32)
    depth = mlp_w1.shape[0]
    for i in range(depth):
        j = i // 2
        if i % 2 == 0:
            ssm_params = (ssm_log_dt[j], ssm_lam_re[j], ssm_lam_im[j], ssm_b_re[j], ssm_b_im[j],
                          ssm_c_re[j], ssm_c_im[j], ssm_d[j])
            x2d = _even_layer(x2d, bsz, s, sb_ssm_w_in[j], ssm_params, ssm_w_glu[j], ssm_b_glu[j],
                              sb_ssm_w_out[j], ln_mix_g[i], ln_mix_b[i])
        else:
            n_in = dsa_w_in.shape[2]
            pad = (-n_in) % LANES
            w_in = jnp.pad(dsa_w_in[j], ((0, 0), (0, pad)))
            x2d = _odd_layer(x2d, bsz, s, w_in, dsa_w_out[j], ln_mix_g[i], ln_mix_b[i])
        x2d = _mlp(x2d, mlp_w1[i].astype(BF16), mlp_w2[i].astype(BF16),
                   ln_ffn_g[i].astype(F32)[None], ln_ffn_b[i].astype(F32)[None])
    return x2d.reshape(bsz, s, d).astype(x.dtype)
```

```python
import functools
import math

import jax
import jax.numpy as jnp
import numpy as np
from jax import lax
from jax.experimental import pallas as pl
from jax.experimental.pallas import tpu as pltpu

F32 = jnp.float32
BF16 = jnp.bfloat16
I32 = jnp.int32

D_MODEL = 1024
HEAD_DIM = 64
SB_WIDTH = 512
SB_HEADS = SB_WIDTH // HEAD_DIM
SSM_WIDTH = 512
SSM_GROUP = 16
SSM_GROUPS = SSM_WIDTH // SSM_GROUP
SSM_STATE = 64
DSA_HEADS = 16
DSA_KV_HEADS = 4
DSA_REP = DSA_HEADS // DSA_KV_HEADS
IDX_HEADS = 8
IDX_DIM = 32
TOPK_MAX = 256
D_FF = 4 * D_MODEL
DEPTH = 4
ALPHA = (2 * DEPTH) ** 0.25
LN_EPS = 1e-5
IDX_SCALE = (IDX_DIM ** -0.5) * (IDX_HEADS ** -0.5)
QK_SCALE = HEAD_DIM ** -0.5

SSM_CHUNK = 16
SUBLANES = 8
BF16_SUBLANES = 16
LANES = 128
VMEM_LIMIT = 56 * 1024 * 1024
EXP_UNDERFLOW = -104.0
NEG_BIG = -1e30
INT_MIN = -2 ** 31
COUNT_ROWS = 64
TR_CHUNK = 256
LOG2E = math.log2(math.e)

_NT = (((1,), (1,)), ((), ()))


def _params(*sem):
    return pltpu.CompilerParams(dimension_semantics=sem, vmem_limit_bytes=VMEM_LIMIT)


def _layer_norm(r, g, b):
    mu = jnp.mean(r, axis=-1, keepdims=True)
    c = r - mu
    var = jnp.mean(c * c, axis=-1, keepdims=True)
    return c * lax.rsqrt(var + LN_EPS) * g + b


def _proj_kernel(x_ref, w_ref, o_ref):
    o_ref[...] = jnp.dot(x_ref[...].astype(BF16), w_ref[...],
                         preferred_element_type=F32).astype(o_ref.dtype)


def _proj(x2d, w, tm=512):
    m, k = x2d.shape
    n = w.shape[1]
    return pl.pallas_call(
        _proj_kernel,
        grid=(m // tm,),
        in_specs=[pl.BlockSpec((tm, k), lambda i: (i, 0)),
                  pl.BlockSpec((k, n), lambda i: (0, 0))],
        out_specs=pl.BlockSpec((tm, n), lambda i: (i, 0)),
        out_shape=jax.ShapeDtypeStruct((m, n), BF16),
        compiler_params=_params("parallel"),
        name="proj",
    )(x2d, w)


def _sb_kernel(q_ref, k_ref, v_ref, o_ref, vt_ref, qm_ref, acc_ref, a_ref, *, tq, tk):
    i = pl.program_id(1)
    s = k_ref.shape[1]
    dh = HEAD_DIM
    heads = q_ref.shape[2] // dh
    pairs = heads // 2

    @pl.when(i == 0)
    def _():
        def transpose_chunk(c, carry):
            c0 = pl.multiple_of(c * TR_CHUNK, TR_CHUNK)
            vt_ref[:, pl.ds(c0, TR_CHUNK)] = v_ref[0, pl.ds(c0, TR_CHUNK), :].astype(F32).T.astype(BF16)
            return carry
        lax.fori_loop(0, s // TR_CHUNK, transpose_chunk, 0)

    lane = lax.broadcasted_iota(I32, (tq, LANES), 1)
    for p in range(pairs):
        qp = q_ref[0, :, p * LANES:(p + 1) * LANES].astype(F32) * QK_SCALE
        qm_ref[2 * p] = jnp.where(lane < dh, qp, 0.0).astype(BF16)
        qm_ref[2 * p + 1] = jnp.where(lane >= dh, qp, 0.0).astype(BF16)
    acc_ref[...] = jnp.zeros_like(acc_ref)
    a_ref[...] = jnp.zeros_like(a_ref)
    rows = lax.broadcasted_iota(I32, (tk, tq), 0)
    tcol = i * tq + lax.broadcasted_iota(I32, (tk, tq), 1)
    tri = lax.broadcasted_iota(I32, (tk, tk), 1) > lax.broadcasted_iota(I32, (tk, tk), 0)
    tri = jnp.where(tri, 1.0, 0.0).astype(BF16)
    tri2 = jnp.concatenate([tri, tri], axis=1)

    def body(carry):
        j, _ = carry
        k0 = pl.multiple_of(j * tk, tk)
        mask = (k0 + rows) < tcol
        hs = range(heads)
        kbs = [k_ref[0, pl.ds(k0, tk), p * LANES:(p + 1) * LANES] for p in range(pairs)]
        zs = [lax.dot_general(kbs[h // 2], qm_ref[h], _NT, preferred_element_type=F32) for h in hs]
        sps = [jnp.log(1.0 + jnp.exp(-jnp.abs(z))) for z in zs]
        log_betas = [jnp.minimum(z, 0.0) - sp for z, sp in zip(zs, sps)]
        l1s = [jnp.where(mask, -jnp.maximum(z, 0.0) - sp, 0.0) for z, sp in zip(zs, sps)]
        his = [l1.astype(BF16) for l1 in l1s]
        los = [(l1 - hi.astype(F32)).astype(BF16) for l1, hi in zip(l1s, his)]
        es = [jnp.dot(tri2, jnp.concatenate([hi, lo], axis=0), preferred_element_type=F32)
              for hi, lo in zip(his, los)]
        a_olds = [a_ref[h] for h in hs]
        ws = [jnp.where(mask, jnp.exp(lb + a + e), 0.0).astype(BF16) for lb, a, e in zip(log_betas, a_olds, es)]
        amax = None
        for h in hs:
            acc_ref[h] += jnp.dot(vt_ref[h * dh:(h + 1) * dh, pl.ds(k0, tk)], ws[h], preferred_element_type=F32)
            a_new = a_olds[h] + jnp.sum(l1s[h], axis=0, keepdims=True)
            a_ref[h] = a_new
            amax = a_new if amax is None else jnp.maximum(amax, a_new)
        return j - 1, jnp.max(amax) > EXP_UNDERFLOW

    j0 = ((i + 1) * tq - 1) // tk
    lax.while_loop(lambda c: jnp.logical_and(c[0] >= 0, c[1]), body, (j0, True))
    for p in range(pairs):
        both = jnp.concatenate([acc_ref[2 * p], acc_ref[2 * p + 1]], axis=0)
        o_ref[0, :, p * LANES:(p + 1) * LANES] = both.T.astype(o_ref.dtype)


def _sb_attention(proj, tq=256, tk=128):
    b, s, _ = proj.shape
    w = SB_WIDTH
    return pl.pallas_call(
        functools.partial(_sb_kernel, tq=tq, tk=tk),
        grid=(b, s // tq),
        in_specs=[pl.BlockSpec((1, tq, w), lambda bi, i: (bi, i, 0)),
                  pl.BlockSpec((1, s, w), lambda bi, i: (bi, 0, 1)),
                  pl.BlockSpec((1, s, w), lambda bi, i: (bi, 0, 2))],
        out_specs=pl.BlockSpec((1, tq, w), lambda bi, i: (bi, i, 0)),
        out_shape=jax.ShapeDtypeStruct((b, s, w), BF16),
        scratch_shapes=[pltpu.VMEM((w, s), BF16),
                        pltpu.VMEM((SB_HEADS, tq, LANES), BF16),
                        pltpu.VMEM((SB_HEADS, HEAD_DIM, tq), F32),
                        pltpu.VMEM((SB_HEADS, 1, tq), F32)],
        compiler_params=_params("parallel", "arbitrary"),
        name="sb_attention",
    )(proj, proj, proj)


def _ssm_kmat_kernel(l_ref, b_ref, o_ref):
    for g in range(l_ref.shape[0]):
        o_ref[g] = jnp.dot(l_ref[g], b_ref[g], precision=lax.Precision.HIGHEST,
                           preferred_element_type=F32)


def _ssm_kmat(lcat, bcat, gb=8):
    g, r, n2 = lcat.shape
    c = bcat.shape[2]
    return pl.pallas_call(
        _ssm_kmat_kernel,
        grid=(g // gb,),
        in_specs=[pl.BlockSpec((gb, r, n2), lambda i: (i, 0, 0)),
                  pl.BlockSpec((gb, n2, c), lambda i: (i, 0, 0))],
        out_specs=pl.BlockSpec((gb, r, c), lambda i: (i, 0, 0)),
        out_shape=jax.ShapeDtypeStruct((g, r, c), F32),
        compiler_params=_params("parallel"),
        name="ssm_kmat",
    )(lcat, bcat)


def _cmul(ar, ai, xr, xi):
    return ar * xr - ai * xi, ar * xi + ai * xr


def _ssm_kernel(u_ref, wk_ref, qr_ref, qi_ref, rr_ref, ri_ref, al_ref, aseg_ref, d_ref, y_ref,
                uf_ref, uc_ref, xr_ref, xi_ref, *, nch, nseg):
    L = SSM_CHUNK
    seg_rows = nch * L
    rows = nch * nseg
    uf_ref[...] = u_ref[0].astype(F32)
    for c in range(nch):
        for s in range(L):
            uc_ref[c * nseg:(c + 1) * nseg, s * LANES:(s + 1) * LANES] = \
                uf_ref[pl.ds(c * L + s, nseg, stride=seg_rows), :]
    ucb = uc_ref[...].astype(BF16)
    xr_ref[...] = jnp.dot(ucb, qr_ref[0], preferred_element_type=F32)
    xi_ref[...] = jnp.dot(ucb, qi_ref[0], preferred_element_type=F32)
    alr, ali = al_ref[0, 0:1, :], al_ref[0, 1:2, :]

    def scan_body(c, carry):
        sr, si = carry
        r0 = pl.multiple_of(c * nseg, nseg)
        inr = xr_ref[pl.ds(r0, nseg), :]
        ini = xi_ref[pl.ds(r0, nseg), :]
        xr_ref[pl.ds(r0, nseg), :] = sr
        xi_ref[pl.ds(r0, nseg), :] = si
        pr, pi = _cmul(alr, ali, sr, si)
        return pr + inr, pi + ini

    zero = jnp.zeros((nseg, xr_ref.shape[1]), F32)
    er, ei = lax.fori_loop(0, nch, scan_body, (zero, zero))

    row = lax.broadcasted_iota(I32, zero.shape, 0)
    cr, ci = zero, zero
    for _ in range(nseg - 1):
        pr, pi = _cmul(aseg_ref[0, 0:1, :], aseg_ref[0, 1:2, :], cr, ci)
        cr = jnp.where(row >= 1, pltpu.roll(er + pr, 1, 0), 0.0)
        ci = jnp.where(row >= 1, pltpu.roll(ei + pi, 1, 0), 0.0)

    def corr_body(c, carry):
        cr, ci = carry
        r0 = pl.multiple_of(c * nseg, nseg)
        xr_ref[pl.ds(r0, nseg), :] += cr
        xi_ref[pl.ds(r0, nseg), :] += ci
        return _cmul(alr, ali, cr, ci)

    lax.fori_loop(0, nch, corr_body, (cr, ci))

    yc = jnp.dot(xr_ref[...].astype(BF16), rr_ref[0], preferred_element_type=F32)
    yc += jnp.dot(xi_ref[...].astype(BF16), ri_ref[0], preferred_element_type=F32)
    for t in range(L):
        lanes = slice(t * LANES, (t + 1) * LANES)
        y_t = jnp.dot(ucb[:, :(t + 1) * LANES], wk_ref[0, (L - 1 - t) * LANES:, :], preferred_element_type=F32)
        y_t += yc[:, lanes] + uc_ref[:, lanes] * d_ref[0]
        for c in range(nch):
            y_ref[0, pl.ds(c * L + t, nseg, stride=seg_rows), :] = y_t[c * nseg:(c + 1) * nseg]


def _ssm_scan(proj, wk, qr, qi, rr, ri, al, aseg, d_t, *, nch, nseg):
    b, s, n = proj.shape
    tiles = SSM_WIDTH // LANES
    first = (n - SSM_WIDTH) // LANES
    rows = nch * nseg
    w = SSM_CHUNK * LANES
    ns = qr.shape[2]
    per_tile = lambda r, c: pl.BlockSpec((1, r, c), lambda j, bi: (j, 0, 0))
    return pl.pallas_call(
        functools.partial(_ssm_kernel, nch=nch, nseg=nseg),
        grid=(tiles, b),
        in_specs=[pl.BlockSpec((1, s, LANES), lambda j, bi: (bi, 0, first + j)),
                  per_tile(w, LANES), per_tile(w, ns), per_tile(w, ns), per_tile(ns, w), per_tile(ns, w),
                  per_tile(2, ns), per_tile(2, ns), per_tile(1, LANES)],
        out_specs=pl.BlockSpec((1, s, LANES), lambda j, bi: (bi, 0, j)),
        out_shape=jax.ShapeDtypeStruct((b, s, SSM_WIDTH), F32),
        scratch_shapes=[pltpu.VMEM((s, LANES), F32), pltpu.VMEM((rows, w), F32),
                        pltpu.VMEM((rows, ns), F32), pltpu.VMEM((rows, ns), F32)],
        compiler_params=_params("parallel", "parallel"),
        name="ssm_scan",
    )(proj, wk, qr, qi, rr, ri, al, aseg, d_t)


def _ssm(proj, log_dt, lam_re, lam_im, b_re, b_im, c_re, c_im, d):
    bsz, s, _ = proj.shape
    L, G, C, N = SSM_CHUNK, SSM_GROUPS, SSM_GROUP, SSM_STATE
    nseg = SUBLANES
    nch = s // (L * nseg)
    dt = jnp.exp(log_dt.astype(F32))[:, None]
    lr = lam_re.astype(F32)
    li = lam_im.astype(F32)

    def apow(tau):
        tau = jnp.asarray(tau, F32)[..., None, None]
        mag = jnp.exp(tau * (lr * dt))
        ang = tau * (li * dt)
        return mag * jnp.cos(ang), mag * jnp.sin(ang)

    pr, pi = apow(np.arange(L + 1))
    den = lr * lr + li * li
    nr = pr[1] - 1.0
    coef_r = ((nr * lr + pi[1] * li) / den)[..., None]
    coef_i = ((pi[1] * lr - nr * li) / den)[..., None]
    br = b_re.astype(F32)
    bi = b_im.astype(F32)
    bbar_r = coef_r * br - coef_i * bi
    bbar_i = coef_r * bi + coef_i * br
    cr = c_re.astype(F32)
    ci = c_im.astype(F32)
    car = cr[None] * pr[:, :, None, :] - ci[None] * pi[:, :, None, :]
    cai = cr[None] * pi[:, :, None, :] + ci[None] * pr[:, :, None, :]
    lcat = jnp.concatenate([car[:L], -cai[:L]], axis=-1)
    lcat = lcat.transpose(1, 0, 2, 3).reshape(G, L * C, 2 * N)
    bcat = jnp.concatenate([bbar_r, bbar_i], axis=1)
    kmat = _ssm_kmat(lcat, bcat).reshape(G, L, C, C)
    gt = LANES // C
    tiles = G // gt
    eye = jnp.eye(gt, dtype=F32)
    k5 = kmat.reshape(tiles, gt, L, C, C)
    wk = jnp.einsum('jgtcd,gh->jtgdhc', k5, eye).reshape(tiles, L, LANES, LANES)
    wk = wk[:, ::-1].reshape(tiles, L * LANES, LANES).astype(BF16)
    prq = pr[L - 1 - np.arange(L)]
    piq = pi[L - 1 - np.arange(L)]
    qr = prq[..., None] * bbar_r[None] - piq[..., None] * bbar_i[None]
    qi = prq[..., None] * bbar_i[None] + piq[..., None] * bbar_r[None]
    tile_q = lambda q: jnp.einsum('sjgnc,gh->jsgchn', q.reshape(L, tiles, gt, N, C), eye).reshape(
        tiles, L * LANES, gt * N).astype(BF16)
    qr, qi = tile_q(qr), tile_q(qi)
    tile_r = lambda r: jnp.einsum('tjgcn,gh->jhntgc', r.reshape(L, tiles, gt, C, N), eye).reshape(
        tiles, gt * N, L * LANES).astype(BF16)
    rr, ri = tile_r(car[1:]), tile_r(-cai[1:])
    sr, si = apow(np.asarray([L * nch]))
    al = jnp.stack([pr[L], pi[L]], axis=1).reshape(tiles, gt, 2, N).transpose(0, 2, 1, 3).reshape(tiles, 2, gt * N)
    aseg = jnp.stack([sr[0], si[0]], axis=1).reshape(tiles, gt, 2, N).transpose(0, 2, 1, 3).reshape(tiles, 2, gt * N)
    d_t = d.astype(F32).reshape(tiles, 1, LANES)
    y = _ssm_scan(proj, wk, qr, qi, rr, ri, al, aseg, d_t, nch=nch, nseg=nseg)
    return y.reshape(bsz * s, G * C)


def _mix_even_kernel(a_ref, y_ref, x_ref, wglu_ref, bglu_ref, wo_ref, g_ref, b_ref, o_ref):
    y = jax.nn.gelu(y_ref[...], approximate=True)
    gate = jnp.dot(y.astype(BF16), wglu_ref[...], preferred_element_type=F32) + bglu_ref[...]
    y = y * (1.0 / (1.0 + jnp.exp(-gate)))
    h = jnp.dot(a_ref[...], wo_ref[:SB_WIDTH, :], preferred_element_type=F32)
    h += jnp.dot(y.astype(BF16), wo_ref[SB_WIDTH:, :], preferred_element_type=F32)
    o_ref[...] = _layer_norm(ALPHA * x_ref[...] + h, g_ref[...], b_ref[...])


def _mix_even(a_out, y, x2d, w_glu, b_glu, w_out, g, b, tm=512):
    m, d = x2d.shape
    row = lambda w: pl.BlockSpec((tm, w), lambda i: (i, 0))
    const = lambda r, c: pl.BlockSpec((r, c), lambda i: (0, 0))
    return pl.pallas_call(
        _mix_even_kernel,
        grid=(m // tm,),
        in_specs=[row(SB_WIDTH), row(SSM_WIDTH), row(d), const(SSM_WIDTH, SSM_WIDTH), const(1, SSM_WIDTH),
                  const(d, d), const(1, d), const(1, d)],
        out_specs=row(d),
        out_shape=jax.ShapeDtypeStruct((m, d), F32),
        compiler_params=_params("parallel"),
        name="mix_even",
    )(a_out, y, x2d, w_glu, b_glu, w_out, g, b)


def _mix_odd_kernel(o_in_ref, x_ref, wo_ref, g_ref, b_ref, o_ref):
    h = jnp.dot(o_in_ref[...], wo_ref[...], preferred_element_type=F32)
    o_ref[...] = _layer_norm(ALPHA * x_ref[...] + h, g_ref[...], b_ref[...])


def _mix_odd(o_in, x2d, w_out, g, b, tm=512):
    m, d = x2d.shape
    row = lambda w: pl.BlockSpec((tm, w), lambda i: (i, 0))
    const = lambda r, c: pl.BlockSpec((r, c), lambda i: (0, 0))
    return pl.pallas_call(
        _mix_odd_kernel,
        grid=(m // tm,),
        in_specs=[row(d), row(d), const(d, d), const(1, d), const(1, d)],
        out_specs=row(d),
        out_shape=jax.ShapeDtypeStruct((m, d), F32),
        compiler_params=_params("parallel"),
        name="mix_odd",
    )(o_in, x2d, w_out, g, b)


def _mlp_kernel(x_ref, w1_ref, w2_ref, g_ref, b_ref, o_ref, *, fchunk):
    x = x_ref[...]
    xb = x.astype(BF16)
    acc = jnp.zeros(x.shape, F32)
    for f in range(0, w1_ref.shape[1], fchunk):
        h = jnp.dot(xb, w1_ref[:, f:f + fchunk], preferred_element_type=F32)
        h = jnp.maximum(h, 0.0)
        acc += jnp.dot((h * h).astype(BF16), w2_ref[f:f + fchunk, :], preferred_element_type=F32)
    o_ref[...] = _layer_norm(ALPHA * x + acc, g_ref[...], b_ref[...])


def _mlp(x2d, w1, w2, g, b, tm=512, fchunk=1024):
    m, d = x2d.shape
    f = w1.shape[1]
    row = pl.BlockSpec((tm, d), lambda i: (i, 0))
    once = lambda r, c: pl.BlockSpec((r, c), lambda i: (0, 0), pipeline_mode=pl.Buffered(1))
    return pl.pallas_call(
        functools.partial(_mlp_kernel, fchunk=fchunk),
        grid=(m // tm,),
        in_specs=[row, once(d, f), once(f, d), once(1, d), once(1, d)],
        out_specs=row,
        out_shape=jax.ShapeDtypeStruct((m, d), F32),
        compiler_params=_params("parallel"),
        name="mlp",
    )(x2d, w1, w2, g, b)


def _bf16_split3(x):
    x = np.asarray(x, np.float32)
    hi = x.astype(BF16).astype(np.float32)
    mid = (x - hi).astype(BF16).astype(np.float32)
    lo = (x - hi - mid).astype(BF16).astype(np.float32)
    return hi, mid, lo


def _alibi_columns(s):
    slopes = (2.0 ** (-8.0 * np.arange(1, DSA_HEADS + 1) / DSA_HEADS)).astype(np.float32)
    slopes = (slopes.astype(np.float64) * LOG2E).astype(np.float32)
    parts = _bf16_split3(slopes)
    assert np.all(parts[0] + parts[1] + parts[2] == slopes)
    q_cols = np.zeros((DSA_HEADS, 1, HEAD_DIM), np.float32)
    q_cols[:, 0, :6] = np.stack(parts * 2, axis=-1)
    pos = np.arange(s)
    k_cols = np.zeros((s, HEAD_DIM), np.float32)
    k_cols[:, 0:3] = (pos // 64 * 64)[:, None]
    k_cols[:, 3:6] = (pos % 64)[:, None]
    return jnp.asarray(q_cols, BF16), jnp.asarray(k_cols, BF16)


def _key_to_float(key):
    bits = key ^ ((key >> 31) & jnp.int32(0x7FFFFFFF))
    return lax.bitcast_convert_type(bits, F32)


def _dsa_kernel(q_ref, k_ref, v_ref, qi_ref, kiw_ref, kiwq_ref, qcols_ref, kcols_ref, o_ref,
                kext_ref, vt_ref, qie_ref, sc_ref, m_ref, acc_ref, qs_ref, bias_ref, xs_ref,
                *, tq, tkb, tk, topk):
    i = pl.program_id(1)
    t0 = i * tq
    s = k_ref.shape[1]
    dh = HEAD_DIM
    kv_heads = k_ref.shape[2] // dh
    n_heads = q_ref.shape[2] // dh
    rep = n_heads // kv_heads
    n_idx_heads = qi_ref.shape[2] // IDX_DIM

    @pl.when(i == 0)
    def _():
        def prep_chunk(c, carry):
            c0 = pl.multiple_of(c * TR_CHUNK, TR_CHUNK)
            kblk = k_ref[0, pl.ds(c0, TR_CHUNK), :].astype(F32)
            vblk_t = v_ref[0, pl.ds(c0, TR_CHUNK), :].astype(F32).T
            for g in range(kv_heads):
                kext_ref[g, pl.ds(c0, TR_CHUNK), 0:dh] = kblk[:, g * dh:(g + 1) * dh].astype(BF16)
                kext_ref[g, pl.ds(c0, TR_CHUNK), dh:2 * dh] = kcols_ref[pl.ds(c0, TR_CHUNK), :]
                vt_ref[g, 0:dh, pl.ds(c0, TR_CHUNK)] = vblk_t[g * dh:(g + 1) * dh].astype(BF16)
            return carry
        lax.fori_loop(0, s // TR_CHUNK, prep_chunk, 0)
        ones_row = lax.broadcasted_iota(I32, (BF16_SUBLANES, s), 0) == 0
        for g in range(kv_heads):
            vt_ref[g, dh:dh + BF16_SUBLANES, :] = jnp.where(ones_row, 1.0, 0.0).astype(BF16)

    qf = q_ref[0].astype(F32) * (QK_SCALE * LOG2E)
    for h in range(n_heads):
        qs_ref[h, :, 0:dh] = qf[:, h * dh:(h + 1) * dh].astype(BF16)
        qs_ref[h, :, dh:2 * dh] = jnp.broadcast_to(qcols_ref[h], (tq, dh))
    qif = qi_ref[0].astype(F32)
    lane = lax.broadcasted_iota(I32, (tq, LANES), 1)
    per_tile = LANES // IDX_DIM
    for h in range(n_idx_heads):
        tile = qif[:, (h // per_tile) * LANES:(h // per_tile + 1) * LANES]
        shift = (h % per_tile) * IDX_DIM
        if shift:
            tile = pltpu.roll(tile, LANES - shift, 1)
        qie_ref[h] = jnp.where(lane < IDX_DIM, tile, 0.0).astype(BF16)
    wt = kiwq_ref[0].astype(F32).T[IDX_DIM:IDX_DIM + n_idx_heads] * IDX_SCALE

    nsb = (t0 + tq + tkb - 1) // tkb
    srow = lax.broadcasted_iota(I32, (tkb, tq), 0)
    tcol = t0 + lax.broadcasted_iota(I32, (tkb, tq), 1)

    def score_body(jb, carry):
        s0 = pl.multiple_of(jb * tkb, tkb)
        kib = kiw_ref[0, pl.ds(s0, tkb), :]
        sc = jnp.zeros((tkb, tq), F32)
        for h in range(n_idx_heads):
            z = lax.dot_general(kib, qie_ref[h], _NT, preferred_element_type=F32)
            sc += wt[h:h + 1, :] * jnp.maximum(z, 0.0)
        sc_ref[pl.ds(s0, tkb), :] = jnp.where(s0 + srow <= tcol, sc, -jnp.inf)
        return carry

    lax.fori_loop(0, nsb, score_body, 0)

    def count(pred):
        def body(jb, acc):
            s0 = pl.multiple_of(jb * tkb, tkb)
            ind = jnp.where(pred(sc_ref[pl.ds(s0, tkb), :], s0 + srow), 1.0, 0.0)
            return acc + jnp.sum(ind.reshape(tkb // COUNT_ROWS, COUNT_ROWS, tq), axis=0)
        acc = lax.fori_loop(0, nsb, body, jnp.zeros((COUNT_ROWS, tq), F32))
        return jnp.sum(acc, axis=0, keepdims=True)

    kf = float(topk)
    c0 = count(lambda sc, sidx: sc >= 0.0)
    ok0 = c0 >= kf
    thr_key0 = jnp.where(ok0, jnp.int32(0), jnp.int32(INT_MIN))
    cnt0 = jnp.where(ok0, c0, 0.0)

    def thr_cond(carry):
        it, _, cnt = carry
        return jnp.logical_and(it < 31, jnp.max(jnp.abs(cnt - kf)) > 0.0)

    def thr_body(carry):
        it, thr_key, cnt = carry
        cand = thr_key | (jnp.int32(1) << (30 - it))
        cand_f = _key_to_float(cand)
        c = count(lambda sc, sidx: sc >= cand_f)
        ok = c >= kf
        return it + 1, jnp.where(ok, cand, thr_key), jnp.where(ok, c, cnt)

    _, thr_key, n_ge = lax.while_loop(thr_cond, thr_body, (jnp.int32(0), thr_key0, cnt0))
    thr = jnp.where(thr_key == INT_MIN, -jnp.inf, _key_to_float(thr_key))
    has_ties = jnp.max(n_ge) > kf
    tie_passes = jnp.where(has_ties, 1, 0)
    n_gt = lax.fori_loop(0, tie_passes, lambda it, c: count(lambda sc, sidx: sc > thr), jnp.zeros((1, tq), F32))
    need = kf - n_gt

    def cut_body(it, cut):
        cand = cut | (jnp.int32(1) << (12 - it))
        c = count(lambda sc, sidx: jnp.logical_and(sc == thr, sidx < cand))
        return jnp.where(c <= need, cand, cut)

    cut = lax.fori_loop(0, jnp.where(has_ties, 13, 0), cut_body, jnp.zeros((1, tq), I32))
    cut = jnp.where(has_ties, cut, jnp.int32(2 ** 30))

    nkb = (t0 + tq + tk - 1) // tk
    srow_k = lax.broadcasted_iota(I32, (tk, tq), 0)
    tcol_k = t0 + lax.broadcasted_iota(I32, (tk, tq), 1)

    def set_block_bias(jb):
        s0 = pl.multiple_of(jb * tk, tk)
        sc = sc_ref[pl.ds(s0, tk), :]
        sidx = s0 + srow_k
        sel = jnp.logical_or(sc > thr, jnp.logical_and(sc == thr, sidx < cut))
        sel = jnp.logical_and(sel, sidx <= tcol_k)
        bias_ref[...] = jnp.where(sel, 0.0, NEG_BIG)
        return s0

    def logits(h, kb):
        return lax.dot_general(kb, qs_ref[h], _NT, preferred_element_type=F32)

    m_ref[...] = jnp.full(m_ref.shape, NEG_BIG, F32)
    acc_ref[...] = jnp.zeros_like(acc_ref)

    def attn_body(jb, carry):
        s0 = set_block_bias(jb)
        maxima = []
        for g in range(kv_heads):
            kb = kext_ref[g, pl.ds(s0, tk), :]
            for r in range(rep):
                h = g * rep + r
                x = logits(h, kb) + bias_ref[...]
                xs_ref[h] = x
                maxima.append(jnp.max(x, axis=0, keepdims=True))
        for g in range(kv_heads):
            vt = vt_ref[g, :, pl.ds(s0, tk)]
            for r in range(rep):
                h = g * rep + r
                m_prev = m_ref[h]
                m_new = jnp.maximum(m_prev, maxima[h])
                p = jnp.exp2(xs_ref[h] - m_new).astype(BF16)
                acc_ref[h] = jnp.exp2(m_prev - m_new) * acc_ref[h] + jnp.dot(
                    vt, p, preferred_element_type=F32)
                m_ref[h] = m_new
        return carry

    lax.fori_loop(0, nkb, attn_body, 0)
    for p in range(n_heads // 2):
        outs = []
        for h in (2 * p, 2 * p + 1):
            acc = acc_ref[h]
            outs.append(acc[:dh] / acc[dh:dh + 1])
        o_ref[0, :, p * LANES:(p + 1) * LANES] = jnp.concatenate(outs, axis=0).T.astype(o_ref.dtype)


def _dsa_attention(proj, topk, tq=256, tkb=512, tk=256):
    b, s, n = proj.shape
    dh = HEAD_DIM
    qw = DSA_HEADS * dh
    kvw = DSA_KV_HEADS * dh
    iw = IDX_HEADS * IDX_DIM
    assert qw % kvw == 0 and (qw + 2 * kvw) % iw == 0 and (qw + 2 * kvw + iw) % LANES == 0
    assert IDX_DIM + IDX_HEADS <= LANES and n == qw + 2 * kvw + iw + LANES
    q_cols, k_cols = _alibi_columns(s)
    tkb = min(tkb, s)
    tk = min(tk, s)
    kiw_block = (qw + 2 * kvw + iw) // LANES
    dv = dh + BF16_SUBLANES
    return pl.pallas_call(
        functools.partial(_dsa_kernel, tq=tq, tkb=tkb, tk=tk, topk=topk),
        grid=(b, s // tq),
        in_specs=[pl.BlockSpec((1, tq, qw), lambda bi, i: (bi, i, 0)),
                  pl.BlockSpec((1, s, kvw), lambda bi, i: (bi, 0, qw // kvw)),
                  pl.BlockSpec((1, s, kvw), lambda bi, i: (bi, 0, qw // kvw + 1)),
                  pl.BlockSpec((1, tq, iw), lambda bi, i: (bi, i, (qw + 2 * kvw) // iw)),
                  pl.BlockSpec((1, s, LANES), lambda bi, i: (bi, 0, kiw_block)),
                  pl.BlockSpec((1, tq, LANES), lambda bi, i: (bi, i, kiw_block)),
                  pl.BlockSpec((DSA_HEADS, 1, dh), lambda bi, i: (0, 0, 0)),
                  pl.BlockSpec((s, dh), lambda bi, i: (0, 0))],
        out_specs=pl.BlockSpec((1, tq, qw), lambda bi, i: (bi, i, 0)),
        out_shape=jax.ShapeDtypeStruct((b, s, qw), BF16),
        scratch_shapes=[pltpu.VMEM((DSA_KV_HEADS, s, 2 * dh), BF16),
                        pltpu.VMEM((DSA_KV_HEADS, dv, s), BF16),
                        pltpu.VMEM((IDX_HEADS, tq, LANES), BF16),
                        pltpu.VMEM((s, tq), F32),
                        pltpu.VMEM((DSA_HEADS, 1, tq), F32),
                        pltpu.VMEM((DSA_HEADS, dv, tq), F32),
                        pltpu.VMEM((DSA_HEADS, tq, 2 * dh), BF16),
                        pltpu.VMEM((tk, tq), F32),
                        pltpu.VMEM((DSA_HEADS, tk, tq), F32)],
        compiler_params=_params("parallel", "arbitrary"),
        name="dsa_attention",
    )(proj, proj, proj, proj, proj, proj, q_cols, k_cols)


def _even_layer(x2d, bsz, s, w_in, ssm_params, w_glu, b_glu, w_out, g_mix, b_mix):
    proj = _proj(x2d, w_in.astype(BF16)).reshape(bsz, s, -1)
    a_out = _sb_attention(proj).reshape(bsz * s, SB_WIDTH)
    y = _ssm(proj, *ssm_params)
    return _mix_even(a_out, y, x2d, w_glu.astype(BF16), b_glu.astype(F32)[None], w_out.astype(BF16),
                     g_mix.astype(F32)[None], b_mix.astype(F32)[None])


def _odd_layer(x2d, bsz, s, w_in, w_out, g_mix, b_mix):
    proj = _proj(x2d, w_in.astype(BF16)).reshape(bsz, s, -1)
    o = _dsa_attention(proj, min(TOPK_MAX, s // 4)).reshape(bsz * s, DSA_HEADS * HEAD_DIM)
    return _mix_odd(o, x2d, w_out.astype(BF16), g_mix.astype(F32)[None], b_mix.astype(F32)[None])


def kernel(x, sb_ssm_w_in, ssm_log_dt, ssm_lam_re, ssm_lam_im, ssm_b_re, ssm_b_im, ssm_c_re, ssm_c_im, ssm_d,
           ssm_w_glu, ssm_b_glu, sb_ssm_w_out, dsa_w_in, dsa_w_out, ln_mix_g, ln_mix_b, ln_ffn_g, ln_ffn_b,
           mlp_w1, mlp_w2):
    bsz, s, d = x.shape
    x2d = x.reshape(bsz * s, d).astype(F32)
    depth = mlp_w1.shape[0]
    for i in range(depth):
        j = i // 2
        if i % 2 == 0:
            ssm_params = (ssm_log_dt[j], ssm_lam_re[j], ssm_lam_im[j], ssm_b_re[j], ssm_b_im[j],
                          ssm_c_re[j], ssm_c_im[j], ssm_d[j])
            x2d = _even_layer(x2d, bsz, s, sb_ssm_w_in[j], ssm_params, ssm_w_glu[j], ssm_b_glu[j],
                              sb_ssm_w_out[j], ln_mix_g[i], ln_mix_b[i])
        else:
            n_in = dsa_w_in.shape[2]
            pad = (-n_in) % LANES
            w_in = jnp.pad(dsa_w_in[j], ((0, 0), (0, pad)))
            x2d = _odd_layer(x2d, bsz, s, w_in, dsa_w_out[j], ln_mix_g[i], ln_mix_b[i])
        x2d = _mlp(x2d, mlp_w1[i].astype(BF16), mlp_w2[i].astype(BF16),
                   ln_ffn_g[i].astype(F32)[None], ln_ffn_b[i].astype(F32)[None])
    return x2d.reshape(bsz, s, d).astype(x.dtype)
```

```python
import functools
import math

import jax
import jax.numpy as jnp
import numpy as np
from jax import lax
from jax.experimental import pallas as pl
from jax.experimental.pallas import tpu as pltpu

F32 = jnp.float32
BF16 = jnp.bfloat16
I32 = jnp.int32

D_MODEL = 1024
HEAD_DIM = 64
SB_WIDTH = 512
SB_HEADS = SB_WIDTH // HEAD_DIM
SSM_WIDTH = 512
SSM_GROUP = 16
SSM_GROUPS = SSM_WIDTH // SSM_GROUP
SSM_STATE = 64
DSA_HEADS = 16
DSA_KV_HEADS = 4
DSA_REP = DSA_HEADS // DSA_KV_HEADS
IDX_HEADS = 8
IDX_DIM = 32
TOPK_MAX = 256
D_FF = 4 * D_MODEL
DEPTH = 4
ALPHA = (2 * DEPTH) ** 0.25
LN_EPS = 1e-5
IDX_SCALE = (IDX_DIM ** -0.5) * (IDX_HEADS ** -0.5)
QK_SCALE = HEAD_DIM ** -0.5

SSM_CHUNK = 16
SUBLANES = 8
BF16_SUBLANES = 16
LANES = 128
VMEM_LIMIT = 56 * 1024 * 1024
EXP2_UNDERFLOW = -150.0
NEG_BIG = -1e30
INT_MIN = -2 ** 31
COUNT_ROWS = 64
TR_CHUNK = 256
LOG2E = math.log2(math.e)
BITS_PER_CHECK = 8

_NT = (((1,), (1,)), ((), ()))


def _params(*sem):
    return pltpu.CompilerParams(dimension_semantics=sem, vmem_limit_bytes=VMEM_LIMIT)


def _layer_norm(r, g, b):
    mu = jnp.mean(r, axis=-1, keepdims=True)
    c = r - mu
    var = jnp.mean(c * c, axis=-1, keepdims=True)
    return c * lax.rsqrt(var + LN_EPS) * g + b


def _proj_kernel(x_ref, w_ref, o_ref):
    o_ref[...] = jnp.dot(x_ref[...].astype(BF16), w_ref[...],
                         preferred_element_type=F32).astype(o_ref.dtype)


def _proj(x2d, w, tm=512):
    m, k = x2d.shape
    n = w.shape[1]
    return pl.pallas_call(
        _proj_kernel,
        grid=(m // tm,),
        in_specs=[pl.BlockSpec((tm, k), lambda i: (i, 0)),
                  pl.BlockSpec((k, n), lambda i: (0, 0))],
        out_specs=pl.BlockSpec((tm, n), lambda i: (i, 0)),
        out_shape=jax.ShapeDtypeStruct((m, n), BF16),
        compiler_params=_params("parallel"),
        name="proj",
    )(x2d, w)


def _sb_kernel(q_ref, k_ref, v_ref, o_ref, vt_ref, qm_ref, acc_ref, a_ref, *, tq, tk):
    i = pl.program_id(1)
    s = k_ref.shape[1]
    dh = HEAD_DIM
    heads = q_ref.shape[2] // dh
    pairs = heads // 2

    @pl.when(i == 0)
    def _():
        def transpose_chunk(c, carry):
            c0 = pl.multiple_of(c * TR_CHUNK, TR_CHUNK)
            vt_ref[:, pl.ds(c0, TR_CHUNK)] = v_ref[0, pl.ds(c0, TR_CHUNK), :].astype(F32).T.astype(BF16)
            return carry
        lax.fori_loop(0, s // TR_CHUNK, transpose_chunk, 0)

    lane = lax.broadcasted_iota(I32, (tq, LANES), 1)
    for p in range(pairs):
        qp = q_ref[0, :, p * LANES:(p + 1) * LANES].astype(F32) * (QK_SCALE * LOG2E)
        qm_ref[2 * p] = jnp.where(lane < dh, qp, 0.0).astype(BF16)
        qm_ref[2 * p + 1] = jnp.where(lane >= dh, qp, 0.0).astype(BF16)
    acc_ref[...] = jnp.zeros_like(acc_ref)
    a_ref[...] = jnp.zeros_like(a_ref)
    rows = lax.broadcasted_iota(I32, (tk, tq), 0)
    tcol = i * tq + lax.broadcasted_iota(I32, (tk, tq), 1)
    tri = lax.broadcasted_iota(I32, (tk, tk), 1) > lax.broadcasted_iota(I32, (tk, tk), 0)
    tri = jnp.where(tri, 1.0, 0.0).astype(BF16)
    tri2 = jnp.concatenate([tri, tri], axis=1)

    def body(carry):
        j, _ = carry
        k0 = pl.multiple_of(j * tk, tk)
        mask = (k0 + rows) < tcol
        hs = range(heads)
        kbs = [k_ref[0, pl.ds(k0, tk), p * LANES:(p + 1) * LANES] for p in range(pairs)]
        zs = [lax.dot_general(kbs[h // 2], qm_ref[h], _NT, preferred_element_type=F32) for h in hs]
        sps = [jnp.log2(1.0 + jnp.exp2(-jnp.abs(z))) for z in zs]
        log_betas = [jnp.minimum(z, 0.0) - sp for z, sp in zip(zs, sps)]
        l1s = [jnp.where(mask, -jnp.maximum(z, 0.0) - sp, 0.0) for z, sp in zip(zs, sps)]
        his = [l1.astype(BF16) for l1 in l1s]
        los = [(l1 - hi.astype(F32)).astype(BF16) for l1, hi in zip(l1s, his)]
        es = [jnp.dot(tri2, jnp.concatenate([hi, lo], axis=0), preferred_element_type=F32)
              for hi, lo in zip(his, los)]
        a_olds = [a_ref[h] for h in hs]
        ws = [jnp.where(mask, jnp.exp2(lb + a + e), 0.0).astype(BF16) for lb, a, e in zip(log_betas, a_olds, es)]
        amax = None
        for h in hs:
            acc_ref[h] += jnp.dot(vt_ref[h * dh:(h + 1) * dh, pl.ds(k0, tk)], ws[h], preferred_element_type=F32)
            a_new = a_olds[h] + jnp.sum(l1s[h], axis=0, keepdims=True)
            a_ref[h] = a_new
            amax = a_new if amax is None else jnp.maximum(amax, a_new)
        return j - 1, jnp.max(amax) > EXP2_UNDERFLOW

    j0 = ((i + 1) * tq - 1) // tk
    lax.while_loop(lambda c: jnp.logical_and(c[0] >= 0, c[1]), body, (j0, True))
    for p in range(pairs):
        both = jnp.concatenate([acc_ref[2 * p], acc_ref[2 * p + 1]], axis=0)
        o_ref[0, :, p * LANES:(p + 1) * LANES] = both.T.astype(o_ref.dtype)


def _sb_attention(proj, tq=256, tk=128):
    b, s, _ = proj.shape
    w = SB_WIDTH
    return pl.pallas_call(
        functools.partial(_sb_kernel, tq=tq, tk=tk),
        grid=(b, s // tq),
        in_specs=[pl.BlockSpec((1, tq, w), lambda bi, i: (bi, i, 0)),
                  pl.BlockSpec((1, s, w), lambda bi, i: (bi, 0, 1)),
                  pl.BlockSpec((1, s, w), lambda bi, i: (bi, 0, 2))],
        out_specs=pl.BlockSpec((1, tq, w), lambda bi, i: (bi, i, 0)),
        out_shape=jax.ShapeDtypeStruct((b, s, w), BF16),
        scratch_shapes=[pltpu.VMEM((w, s), BF16),
                        pltpu.VMEM((SB_HEADS, tq, LANES), BF16),
                        pltpu.VMEM((SB_HEADS, HEAD_DIM, tq), F32),
                        pltpu.VMEM((SB_HEADS, 1, tq), F32)],
        compiler_params=_params("parallel", "arbitrary"),
        name="sb_attention",
    )(proj, proj, proj)


def _ssm_kmat_kernel(l_ref, b_ref, o_ref):
    for g in range(l_ref.shape[0]):
        o_ref[g] = jnp.dot(l_ref[g], b_ref[g], precision=lax.Precision.HIGHEST,
                           preferred_element_type=F32)


def _ssm_kmat(lcat, bcat, gb=8):
    g, r, n2 = lcat.shape
    c = bcat.shape[2]
    return pl.pallas_call(
        _ssm_kmat_kernel,
        grid=(g // gb,),
        in_specs=[pl.BlockSpec((gb, r, n2), lambda i: (i, 0, 0)),
                  pl.BlockSpec((gb, n2, c), lambda i: (i, 0, 0))],
        out_specs=pl.BlockSpec((gb, r, c), lambda i: (i, 0, 0)),
        out_shape=jax.ShapeDtypeStruct((g, r, c), F32),
        compiler_params=_params("parallel"),
        name="ssm_kmat",
    )(lcat, bcat)


def _cmul(ar, ai, xr, xi):
    return ar * xr - ai * xi, ar * xi + ai * xr


def _ssm_kernel(u_ref, wk_ref, qr_ref, qi_ref, rr_ref, ri_ref, al_ref, aseg_ref, d_ref, y_ref,
                uf_ref, uc_ref, xr_ref, xi_ref, *, nch, nseg):
    L = SSM_CHUNK
    seg_rows = nch * L
    rows = nch * nseg
    uf_ref[...] = u_ref[0].astype(F32)
    for c in range(nch):
        for s in range(L):
            uc_ref[c * nseg:(c + 1) * nseg, s * LANES:(s + 1) * LANES] = \
                uf_ref[pl.ds(c * L + s, nseg, stride=seg_rows), :]
    ucb = uc_ref[...].astype(BF16)
    xr_ref[...] = jnp.dot(ucb, qr_ref[0], preferred_element_type=F32)
    xi_ref[...] = jnp.dot(ucb, qi_ref[0], preferred_element_type=F32)
    alr, ali = al_ref[0, 0:1, :], al_ref[0, 1:2, :]

    def scan_body(c, carry):
        sr, si = carry
        r0 = pl.multiple_of(c * nseg, nseg)
        inr = xr_ref[pl.ds(r0, nseg), :]
        ini = xi_ref[pl.ds(r0, nseg), :]
        xr_ref[pl.ds(r0, nseg), :] = sr
        xi_ref[pl.ds(r0, nseg), :] = si
        pr, pi = _cmul(alr, ali, sr, si)
        return pr + inr, pi + ini

    zero = jnp.zeros((nseg, xr_ref.shape[1]), F32)
    er, ei = lax.fori_loop(0, nch, scan_body, (zero, zero))

    row = lax.broadcasted_iota(I32, zero.shape, 0)
    cr, ci = zero, zero
    for _ in range(nseg - 1):
        pr, pi = _cmul(aseg_ref[0, 0:1, :], aseg_ref[0, 1:2, :], cr, ci)
        cr = jnp.where(row >= 1, pltpu.roll(er + pr, 1, 0), 0.0)
        ci = jnp.where(row >= 1, pltpu.roll(ei + pi, 1, 0), 0.0)

    def corr_body(c, carry):
        cr, ci = carry
        r0 = pl.multiple_of(c * nseg, nseg)
        xr_ref[pl.ds(r0, nseg), :] += cr
        xi_ref[pl.ds(r0, nseg), :] += ci
        return _cmul(alr, ali, cr, ci)

    lax.fori_loop(0, nch, corr_body, (cr, ci))

    yc = jnp.dot(xr_ref[...].astype(BF16), rr_ref[0], preferred_element_type=F32)
    yc += jnp.dot(xi_ref[...].astype(BF16), ri_ref[0], preferred_element_type=F32)
    for t in range(L):
        lanes = slice(t * LANES, (t + 1) * LANES)
        y_t = jnp.dot(ucb[:, :(t + 1) * LANES], wk_ref[0, (L - 1 - t) * LANES:, :], preferred_element_type=F32)
        y_t += yc[:, lanes] + uc_ref[:, lanes] * d_ref[0]
        for c in range(nch):
            y_ref[0, pl.ds(c * L + t, nseg, stride=seg_rows), :] = y_t[c * nseg:(c + 1) * nseg]


def _ssm_scan(proj, wk, qr, qi, rr, ri, al, aseg, d_t, *, nch, nseg):
    b, s, n = proj.shape
    tiles = SSM_WIDTH // LANES
    first = (n - SSM_WIDTH) // LANES
    rows = nch * nseg
    w = SSM_CHUNK * LANES
    ns = qr.shape[2]
    per_tile = lambda r, c: pl.BlockSpec((1, r, c), lambda j, bi: (j, 0, 0))
    return pl.pallas_call(
        functools.partial(_ssm_kernel, nch=nch, nseg=nseg),
        grid=(tiles, b),
        in_specs=[pl.BlockSpec((1, s, LANES), lambda j, bi: (bi, 0, first + j)),
                  per_tile(w, LANES), per_tile(w, ns), per_tile(w, ns), per_tile(ns, w), per_tile(ns, w),
                  per_tile(2, ns), per_tile(2, ns), per_tile(1, LANES)],
        out_specs=pl.BlockSpec((1, s, LANES), lambda j, bi: (bi, 0, j)),
        out_shape=jax.ShapeDtypeStruct((b, s, SSM_WIDTH), F32),
        scratch_shapes=[pltpu.VMEM((s, LANES), F32), pltpu.VMEM((rows, w), F32),
                        pltpu.VMEM((rows, ns), F32), pltpu.VMEM((rows, ns), F32)],
        compiler_params=_params("parallel", "parallel"),
        name="ssm_scan",
    )(proj, wk, qr, qi, rr, ri, al, aseg, d_t)


def _ssm_operators(log_dt, lam_re, lam_im, b_re, b_im, c_re, c_im, d, nch):
    L, C, N = SSM_CHUNK, SSM_GROUP, SSM_STATE
    G = log_dt.shape[0]
    dt = jnp.exp(log_dt.astype(F32))[:, None]
    lr = lam_re.astype(F32)
    li = lam_im.astype(F32)

    def apow(tau):
        tau = jnp.asarray(tau, F32)[..., None, None]
        mag = jnp.exp(tau * (lr * dt))
        ang = tau * (li * dt)
        return mag * jnp.cos(ang), mag * jnp.sin(ang)

    pr, pi = apow(np.arange(L + 1))
    den = lr * lr + li * li
    nr = pr[1] - 1.0
    coef_r = ((nr * lr + pi[1] * li) / den)[..., None]
    coef_i = ((pi[1] * lr - nr * li) / den)[..., None]
    br = b_re.astype(F32)
    bi = b_im.astype(F32)
    bbar_r = coef_r * br - coef_i * bi
    bbar_i = coef_r * bi + coef_i * br
    cr = c_re.astype(F32)
    ci = c_im.astype(F32)
    car = cr[None] * pr[:, :, None, :] - ci[None] * pi[:, :, None, :]
    cai = cr[None] * pi[:, :, None, :] + ci[None] * pr[:, :, None, :]
    lcat = jnp.concatenate([car[:L], -cai[:L]], axis=-1)
    lcat = lcat.transpose(1, 0, 2, 3).reshape(G, L * C, 2 * N)
    bcat = jnp.concatenate([bbar_r, bbar_i], axis=1)
    kmat = _ssm_kmat(lcat, bcat).reshape(G, L, C, C)
    gt = LANES // C
    tiles = G // gt
    eye = jnp.eye(gt, dtype=F32)
    k5 = kmat.reshape(tiles, gt, L, C, C)
    wk = jnp.einsum('jgtcd,gh->jtgdhc', k5, eye).reshape(tiles, L, LANES, LANES)
    wk = wk[:, ::-1].reshape(tiles, L * LANES, LANES).astype(BF16)
    prq = pr[L - 1 - np.arange(L)]
    piq = pi[L - 1 - np.arange(L)]
    qr = prq[..., None] * bbar_r[None] - piq[..., None] * bbar_i[None]
    qi = prq[..., None] * bbar_i[None] + piq[..., None] * bbar_r[None]
    tile_q = lambda q: jnp.einsum('sjgnc,gh->jsgchn', q.reshape(L, tiles, gt, N, C), eye).reshape(
        tiles, L * LANES, gt * N).astype(BF16)
    qr, qi = tile_q(qr), tile_q(qi)
    tile_r = lambda r: jnp.einsum('tjgcn,gh->jhntgc', r.reshape(L, tiles, gt, C, N), eye).reshape(
        tiles, gt * N, L * LANES).astype(BF16)
    rr, ri = tile_r(car[1:]), tile_r(-cai[1:])
    sr, si = apow(np.asarray([L * nch]))
    al = jnp.stack([pr[L], pi[L]], axis=1).reshape(tiles, gt, 2, N).transpose(0, 2, 1, 3).reshape(tiles, 2, gt * N)
    aseg = jnp.stack([sr[0], si[0]], axis=1).reshape(tiles, gt, 2, N).transpose(0, 2, 1, 3).reshape(tiles, 2, gt * N)
    d_t = d.astype(F32).reshape(tiles, 1, LANES)
    return wk, qr, qi, rr, ri, al, aseg, d_t


def _mix_even_kernel(a_ref, y_ref, x_ref, wglu_ref, bglu_ref, wo_ref, g_ref, b_ref, o_ref):
    y = jax.nn.gelu(y_ref[...], approximate=True)
    gate = jnp.dot(y.astype(BF16), wglu_ref[...], preferred_element_type=F32) + bglu_ref[...]
    y = y * (1.0 / (1.0 + jnp.exp(-gate)))
    h = jnp.dot(a_ref[...], wo_ref[:SB_WIDTH, :], preferred_element_type=F32)
    h += jnp.dot(y.astype(BF16), wo_ref[SB_WIDTH:, :], preferred_element_type=F32)
    o_ref[...] = _layer_norm(ALPHA * x_ref[...] + h, g_ref[...], b_ref[...])


def _mix_even(a_out, y, x2d, w_glu, b_glu, w_out, g, b, tm=512):
    m, d = x2d.shape
    row = lambda w: pl.BlockSpec((tm, w), lambda i: (i, 0))
    const = lambda r, c: pl.BlockSpec((r, c), lambda i: (0, 0))
    return pl.pallas_call(
        _mix_even_kernel,
        grid=(m // tm,),
        in_specs=[row(SB_WIDTH), row(SSM_WIDTH), row(d), const(SSM_WIDTH, SSM_WIDTH), const(1, SSM_WIDTH),
                  const(d, d), const(1, d), const(1, d)],
        out_specs=row(d),
        out_shape=jax.ShapeDtypeStruct((m, d), F32),
        compiler_params=_params("parallel"),
        name="mix_even",
    )(a_out, y, x2d, w_glu, b_glu, w_out, g, b)


def _mix_odd_kernel(o_in_ref, x_ref, wo_ref, g_ref, b_ref, o_ref):
    h = jnp.dot(o_in_ref[...], wo_ref[...], preferred_element_type=F32)
    o_ref[...] = _layer_norm(ALPHA * x_ref[...] + h, g_ref[...], b_ref[...])


def _mix_odd(o_in, x2d, w_out, g, b, tm=512):
    m, d = x2d.shape
    row = lambda w: pl.BlockSpec((tm, w), lambda i: (i, 0))
    const = lambda r, c: pl.BlockSpec((r, c), lambda i: (0, 0))
    return pl.pallas_call(
        _mix_odd_kernel,
        grid=(m // tm,),
        in_specs=[row(d), row(d), const(d, d), const(1, d), const(1, d)],
        out_specs=row(d),
        out_shape=jax.ShapeDtypeStruct((m, d), F32),
        compiler_params=_params("parallel"),
        name="mix_odd",
    )(o_in, x2d, w_out, g, b)


def _mlp_kernel(x_ref, w1_ref, w2_ref, g_ref, b_ref, o_ref, *, fchunk):
    x = x_ref[...]
    xb = x.astype(BF16)
    acc = jnp.zeros(x.shape, F32)
    for f in range(0, w1_ref.shape[1], fchunk):
        h = jnp.dot(xb, w1_ref[:, f:f + fchunk], preferred_element_type=F32)
        h = jnp.maximum(h, 0.0)
        acc += jnp.dot((h * h).astype(BF16), w2_ref[f:f + fchunk, :], preferred_element_type=F32)
    o_ref[...] = _layer_norm(ALPHA * x + acc, g_ref[...], b_ref[...])


def _mlp(x2d, w1, w2, g, b, tm=512, fchunk=1024):
    m, d = x2d.shape
    f = w1.shape[1]
    row = pl.BlockSpec((tm, d), lambda i: (i, 0))
    once = lambda r, c: pl.BlockSpec((r, c), lambda i: (0, 0), pipeline_mode=pl.Buffered(1))
    return pl.pallas_call(
        functools.partial(_mlp_kernel, fchunk=fchunk),
        grid=(m // tm,),
        in_specs=[row, once(d, f), once(f, d), once(1, d), once(1, d)],
        out_specs=row,
        out_shape=jax.ShapeDtypeStruct((m, d), F32),
        compiler_params=_params("parallel"),
        name="mlp",
    )(x2d, w1, w2, g, b)


def _bf16_split3(x):
    x = np.asarray(x, np.float32)
    hi = x.astype(BF16).astype(np.float32)
    mid = (x - hi).astype(BF16).astype(np.float32)
    lo = (x - hi - mid).astype(BF16).astype(np.float32)
    return hi, mid, lo


def _alibi_columns(s):
    slopes = (2.0 ** (-8.0 * np.arange(1, DSA_HEADS + 1) / DSA_HEADS)).astype(np.float32)
    slopes = (slopes.astype(np.float64) * LOG2E).astype(np.float32)
    parts = _bf16_split3(slopes)
    assert np.all(parts[0] + parts[1] + parts[2] == slopes)
    q_cols = np.zeros((DSA_HEADS, 1, HEAD_DIM), np.float32)
    q_cols[:, 0, :6] = np.stack(parts * 2, axis=-1)
    pos = np.arange(s)
    k_cols = np.zeros((s, HEAD_DIM), np.float32)
    k_cols[:, 0:3] = (pos // 64 * 64)[:, None]
    k_cols[:, 3:6] = (pos % 64)[:, None]
    return jnp.asarray(q_cols, BF16), jnp.asarray(k_cols, BF16)


def _key_to_float(key):
    bits = key ^ ((key >> 31) & jnp.int32(0x7FFFFFFF))
    return lax.bitcast_convert_type(bits, F32)


def _dsa_kernel(q_ref, k_ref, v_ref, qi_ref, kiw_ref, kiwq_ref, qcols_ref, kcols_ref, o_ref,
                kext_ref, vt_ref, qie_ref, sc_ref, m_ref, acc_ref, qs_ref, bias_ref, xs_ref,
                *, tq, tkb, tk, topk):
    i = pl.program_id(1)
    t0 = i * tq
    s = k_ref.shape[1]
    dh = HEAD_DIM
    kv_heads = k_ref.shape[2] // dh
    n_heads = q_ref.shape[2] // dh
    rep = n_heads // kv_heads
    n_idx_heads = qi_ref.shape[2] // IDX_DIM

    @pl.when(i == 0)
    def _():
        def prep_chunk(c, carry):
            c0 = pl.multiple_of(c * TR_CHUNK, TR_CHUNK)
            kblk = k_ref[0, pl.ds(c0, TR_CHUNK), :].astype(F32)
            vblk_t = v_ref[0, pl.ds(c0, TR_CHUNK), :].astype(F32).T
            for g in range(kv_heads):
                kext_ref[g, pl.ds(c0, TR_CHUNK), 0:dh] = kblk[:, g * dh:(g + 1) * dh].astype(BF16)
                kext_ref[g, pl.ds(c0, TR_CHUNK), dh:2 * dh] = kcols_ref[pl.ds(c0, TR_CHUNK), :]
                vt_ref[g, 0:dh, pl.ds(c0, TR_CHUNK)] = vblk_t[g * dh:(g + 1) * dh].astype(BF16)
            return carry
        lax.fori_loop(0, s // TR_CHUNK, prep_chunk, 0)
        ones_row = lax.broadcasted_iota(I32, (BF16_SUBLANES, s), 0) == 0
        for g in range(kv_heads):
            vt_ref[g, dh:dh + BF16_SUBLANES, :] = jnp.where(ones_row, 1.0, 0.0).astype(BF16)

    qf = q_ref[0].astype(F32) * (QK_SCALE * LOG2E)
    for h in range(n_heads):
        qs_ref[h, :, 0:dh] = qf[:, h * dh:(h + 1) * dh].astype(BF16)
        qs_ref[h, :, dh:2 * dh] = jnp.broadcast_to(qcols_ref[h], (tq, dh))
    qif = qi_ref[0].astype(F32)
    lane = lax.broadcasted_iota(I32, (tq, LANES), 1)
    per_tile = LANES // IDX_DIM
    for h in range(n_idx_heads):
        tile = qif[:, (h // per_tile) * LANES:(h // per_tile + 1) * LANES]
        shift = (h % per_tile) * IDX_DIM
        if shift:
            tile = pltpu.roll(tile, LANES - shift, 1)
        qie_ref[h] = jnp.where(lane < IDX_DIM, tile, 0.0).astype(BF16)
    wt = kiwq_ref[0].astype(F32).T[IDX_DIM:IDX_DIM + n_idx_heads] * IDX_SCALE

    nsb = (t0 + tq + tkb - 1) // tkb
    srow = lax.broadcasted_iota(I32, (tkb, tq), 0)
    tcol = t0 + lax.broadcasted_iota(I32, (tkb, tq), 1)

    def score_body(jb, carry):
        s0 = pl.multiple_of(jb * tkb, tkb)
        kib = kiw_ref[0, pl.ds(s0, tkb), :]
        sc = jnp.zeros((tkb, tq), F32)
        for h in range(n_idx_heads):
            z = lax.dot_general(kib, qie_ref[h], _NT, preferred_element_type=F32)
            sc += wt[h:h + 1, :] * jnp.maximum(z, 0.0)
        sc_ref[pl.ds(s0, tkb), :] = jnp.where(s0 + srow <= tcol, sc, -jnp.inf)
        return carry

    lax.fori_loop(0, nsb, score_body, 0)

    def count(pred):
        def body(jb, acc):
            s0 = pl.multiple_of(jb * tkb, tkb)
            ind = jnp.where(pred(sc_ref[pl.ds(s0, tkb), :], s0 + srow), 1.0, 0.0)
            return acc + jnp.sum(ind.reshape(tkb // COUNT_ROWS, COUNT_ROWS, tq), axis=0)
        acc = lax.fori_loop(0, nsb, body, jnp.zeros((COUNT_ROWS, tq), F32))
        return jnp.sum(acc, axis=0, keepdims=True)

    kf = float(topk)
    c0 = count(lambda sc, sidx: sc >= 0.0)
    ok0 = c0 >= kf
    thr_key0 = jnp.where(ok0, jnp.int32(0), jnp.int32(INT_MIN))
    cnt0 = jnp.where(ok0, c0, 0.0)

    def thr_cond(carry):
        it, _, cnt = carry
        return jnp.logical_and(it < 31, jnp.max(jnp.abs(cnt - kf)) > 0.0)

    def bit_body(it, carry):
        thr_key, cnt = carry
        cand = thr_key | (jnp.int32(1) << (30 - it))
        cand_f = _key_to_float(cand)
        c = count(lambda sc, sidx: sc >= cand_f)
        ok = c >= kf
        return jnp.where(ok, cand, thr_key), jnp.where(ok, c, cnt)

    def thr_body(carry):
        it, thr_key, cnt = carry
        nxt = jnp.minimum(it + BITS_PER_CHECK, 31)
        thr_key, cnt = lax.fori_loop(it, nxt, bit_body, (thr_key, cnt))
        return nxt, thr_key, cnt

    _, thr_key, n_ge = lax.while_loop(thr_cond, thr_body, (jnp.int32(0), thr_key0, cnt0))
    thr = jnp.where(thr_key == INT_MIN, -jnp.inf, _key_to_float(thr_key))
    has_ties = jnp.max(n_ge) > kf
    tie_passes = jnp.where(has_ties, 1, 0)
    n_gt = lax.fori_loop(0, tie_passes, lambda it, c: count(lambda sc, sidx: sc > thr), jnp.zeros((1, tq), F32))
    need = kf - n_gt

    def cut_body(it, cut):
        cand = cut | (jnp.int32(1) << (12 - it))
        c = count(lambda sc, sidx: jnp.logical_and(sc == thr, sidx < cand))
        return jnp.where(c <= need, cand, cut)

    cut = lax.fori_loop(0, jnp.where(has_ties, 13, 0), cut_body, jnp.zeros((1, tq), I32))
    cut = jnp.where(has_ties, cut, jnp.int32(2 ** 30))

    nkb = (t0 + tq + tk - 1) // tk
    srow_k = lax.broadcasted_iota(I32, (tk, tq), 0)
    tcol_k = t0 + lax.broadcasted_iota(I32, (tk, tq), 1)

    def set_block_bias(jb):
        s0 = pl.multiple_of(jb * tk, tk)
        sc = sc_ref[pl.ds(s0, tk), :]
        sidx = s0 + srow_k
        sel = jnp.logical_or(sc > thr, jnp.logical_and(sc == thr, sidx < cut))
        sel = jnp.logical_and(sel, sidx <= tcol_k)
        bias_ref[...] = jnp.where(sel, 0.0, NEG_BIG)
        return s0

    def logits(h, kb):
        return lax.dot_general(kb, qs_ref[h], _NT, preferred_element_type=F32)

    m_ref[...] = jnp.full(m_ref.shape, NEG_BIG, F32)
    acc_ref[...] = jnp.zeros_like(acc_ref)

    def attn_body(jb, carry):
        s0 = set_block_bias(jb)
        maxima = []
        for g in range(kv_heads):
            kb = kext_ref[g, pl.ds(s0, tk), :]
            for r in range(rep):
                h = g * rep + r
                x = logits(h, kb) + bias_ref[...]
                xs_ref[h] = x
                maxima.append(jnp.max(x, axis=0, keepdims=True))
        for g in range(kv_heads):
            vt = vt_ref[g, :, pl.ds(s0, tk)]
            for r in range(rep):
                h = g * rep + r
                m_prev = m_ref[h]
                m_new = jnp.maximum(m_prev, maxima[h])
                p = jnp.exp2(xs_ref[h] - m_new).astype(BF16)
                acc_ref[h] = jnp.exp2(m_prev - m_new) * acc_ref[h] + jnp.dot(
                    vt, p, preferred_element_type=F32)
                m_ref[h] = m_new
        return carry

    lax.fori_loop(0, nkb, attn_body, 0)
    for p in range(n_heads // 2):
        outs = []
        for h in (2 * p, 2 * p + 1):
            acc = acc_ref[h]
            outs.append(acc[:dh] / acc[dh:dh + 1])
        o_ref[0, :, p * LANES:(p + 1) * LANES] = jnp.concatenate(outs, axis=0).T.astype(o_ref.dtype)


def _dsa_attention(proj, topk, tq=256, tkb=512, tk=256):
    b, s, n = proj.shape
    dh = HEAD_DIM
    qw = DSA_HEADS * dh
    kvw = DSA_KV_HEADS * dh
    iw = IDX_HEADS * IDX_DIM
    assert qw % kvw == 0 and (qw + 2 * kvw) % iw == 0 and (qw + 2 * kvw + iw) % LANES == 0
    assert IDX_DIM + IDX_HEADS <= LANES and n == qw + 2 * kvw + iw + LANES
    q_cols, k_cols = _alibi_columns(s)
    tkb = min(tkb, s)
    tk = min(tk, s)
    kiw_block = (qw + 2 * kvw + iw) // LANES
    dv = dh + BF16_SUBLANES
    return pl.pallas_call(
        functools.partial(_dsa_kernel, tq=tq, tkb=tkb, tk=tk, topk=topk),
        grid=(b, s // tq),
        in_specs=[pl.BlockSpec((1, tq, qw), lambda bi, i: (bi, i, 0)),
                  pl.BlockSpec((1, s, kvw), lambda bi, i: (bi, 0, qw // kvw)),
                  pl.BlockSpec((1, s, kvw), lambda bi, i: (bi, 0, qw // kvw + 1)),
                  pl.BlockSpec((1, tq, iw), lambda bi, i: (bi, i, (qw + 2 * kvw) // iw)),
                  pl.BlockSpec((1, s, LANES), lambda bi, i: (bi, 0, kiw_block)),
                  pl.BlockSpec((1, tq, LANES), lambda bi, i: (bi, i, kiw_block)),
                  pl.BlockSpec((DSA_HEADS, 1, dh), lambda bi, i: (0, 0, 0)),
                  pl.BlockSpec((s, dh), lambda bi, i: (0, 0))],
        out_specs=pl.BlockSpec((1, tq, qw), lambda bi, i: (bi, i, 0)),
        out_shape=jax.ShapeDtypeStruct((b, s, qw), BF16),
        scratch_shapes=[pltpu.VMEM((DSA_KV_HEADS, s, 2 * dh), BF16),
                        pltpu.VMEM((DSA_KV_HEADS, dv, s), BF16),
                        pltpu.VMEM((IDX_HEADS, tq, LANES), BF16),
                        pltpu.VMEM((s, tq), F32),
                        pltpu.VMEM((DSA_HEADS, 1, tq), F32),
                        pltpu.VMEM((DSA_HEADS, dv, tq), F32),
                        pltpu.VMEM((DSA_HEADS, tq, 2 * dh), BF16),
                        pltpu.VMEM((tk, tq), F32),
                        pltpu.VMEM((DSA_HEADS, tk, tq), F32)],
        compiler_params=_params("parallel", "arbitrary"),
        name="dsa_attention",
    )(proj, proj, proj, proj, proj, proj, q_cols, k_cols)


def _even_layer(x2d, bsz, s, w_in, ssm_ops, w_glu, b_glu, w_out, g_mix, b_mix):
    proj = _proj(x2d, w_in.astype(BF16)).reshape(bsz, s, -1)
    a_out = _sb_attention(proj).reshape(bsz * s, SB_WIDTH)
    y = _ssm_scan(proj, *ssm_ops, nch=s // (SSM_CHUNK * SUBLANES), nseg=SUBLANES).reshape(bsz * s, SSM_WIDTH)
    return _mix_even(a_out, y, x2d, w_glu.astype(BF16), b_glu.astype(F32)[None], w_out.astype(BF16),
                     g_mix.astype(F32)[None], b_mix.astype(F32)[None])


def _odd_layer(x2d, bsz, s, w_in, w_out, g_mix, b_mix):
    proj = _proj(x2d, w_in.astype(BF16)).reshape(bsz, s, -1)
    o = _dsa_attention(proj, min(TOPK_MAX, s // 4)).reshape(bsz * s, DSA_HEADS * HEAD_DIM)
    return _mix_odd(o, x2d, w_out.astype(BF16), g_mix.astype(F32)[None], b_mix.astype(F32)[None])


def kernel(x, sb_ssm_w_in, ssm_log_dt, ssm_lam_re, ssm_lam_im, ssm_b_re, ssm_b_im, ssm_c_re, ssm_c_im, ssm_d,
           ssm_w_glu, ssm_b_glu, sb_ssm_w_out, dsa_w_in, dsa_w_out, ln_mix_g, ln_mix_b, ln_ffn_g, ln_ffn_b,
           mlp_w1, mlp_w2):
    bsz, s, d = x.shape
    x2d = x.reshape(bsz * s, d).astype(F32)
    depth = mlp_w1.shape[0]
    stacked = [p.reshape((-1,) + p.shape[2:]) for p in
               (ssm_log_dt, ssm_lam_re, ssm_lam_im, ssm_b_re, ssm_b_im, ssm_c_re, ssm_c_im, ssm_d)]
    ssm_ops = _ssm_operators(*stacked, nch=s // (SSM_CHUNK * SUBLANES))
    tiles = SSM_WIDTH // LANES
    for i in range(depth):
        j = i // 2
        if i % 2 == 0:
            layer_ops = tuple(op[j * tiles:(j + 1) * tiles] for op in ssm_ops)
            x2d = _even_layer(x2d, bsz, s, sb_ssm_w_in[j], layer_ops, ssm_w_glu[j], ssm_b_glu[j],
                              sb_ssm_w_out[j], ln_mix_g[i], ln_mix_b[i])
        else:
            n_in = dsa_w_in.shape[2]
            pad = (-n_in) % LANES
            w_in = jnp.pad(dsa_w_in[j], ((0, 0), (0, pad)))
            x2d = _odd_layer(x2d, bsz, s, w_in, dsa_w_out[j], ln_mix_g[i], ln_mix_b[i])
        x2d = _mlp(x2d, mlp_w1[i].astype(BF16), mlp_w2[i].astype(BF16),
                   ln_ffn_g[i].astype(F32)[None], ln_ffn_b[i].astype(F32)[None])
    return x2d.reshape(bsz, s, d).astype(x.dtype)
```

```python
import functools
import math

import jax
import jax.numpy as jnp
import numpy as np
from jax import lax
from jax.experimental import pallas as pl
from jax.experimental.pallas import tpu as pltpu

F32 = jnp.float32
BF16 = jnp.bfloat16
I32 = jnp.int32

D_MODEL = 1024
HEAD_DIM = 64
SB_WIDTH = 512
SB_HEADS = SB_WIDTH // HEAD_DIM
SSM_WIDTH = 512
SSM_GROUP = 16
SSM_GROUPS = SSM_WIDTH // SSM_GROUP
SSM_STATE = 64
DSA_HEADS = 16
DSA_KV_HEADS = 4
DSA_REP = DSA_HEADS // DSA_KV_HEADS
IDX_HEADS = 8
IDX_DIM = 32
TOPK_MAX = 256
D_FF = 4 * D_MODEL
DEPTH = 4
ALPHA = (2 * DEPTH) ** 0.25
LN_EPS = 1e-5
IDX_SCALE = (IDX_DIM ** -0.5) * (IDX_HEADS ** -0.5)
QK_SCALE = HEAD_DIM ** -0.5

SSM_CHUNK = 16
SUBLANES = 8
BF16_SUBLANES = 16
LANES = 128
VMEM_LIMIT = 56 * 1024 * 1024
EXP2_UNDERFLOW = -150.0
NEG_BIG = -1e30
INT_MIN = -2 ** 31
COUNT_ROWS = 64
TR_CHUNK = 256
LOG2E = math.log2(math.e)
MLP_FCHUNK = 1024
BITS_PER_CHECK = 8

_NT = (((1,), (1,)), ((), ()))


def _params(*sem):
    return pltpu.CompilerParams(dimension_semantics=sem, vmem_limit_bytes=VMEM_LIMIT)


def _layer_norm(r, g, b):
    mu = jnp.mean(r, axis=-1, keepdims=True)
    c = r - mu
    var = jnp.mean(c * c, axis=-1, keepdims=True)
    return c * lax.rsqrt(var + LN_EPS) * g + b


def _proj_kernel(x_ref, w_ref, o_ref):
    o_ref[...] = jnp.dot(x_ref[...].astype(BF16), w_ref[...],
                         preferred_element_type=F32).astype(o_ref.dtype)


def _proj(x2d, w, tm=512):
    m, k = x2d.shape
    n = w.shape[1]
    return pl.pallas_call(
        _proj_kernel,
        grid=(m // tm,),
        in_specs=[pl.BlockSpec((tm, k), lambda i: (i, 0)),
                  pl.BlockSpec((k, n), lambda i: (0, 0))],
        out_specs=pl.BlockSpec((tm, n), lambda i: (i, 0)),
        out_shape=jax.ShapeDtypeStruct((m, n), BF16),
        compiler_params=_params("parallel"),
        name="proj",
    )(x2d, w)


def _sb_kernel(q_ref, k_ref, v_ref, o_ref, vt_ref, qm_ref, acc_ref, a_ref, *, tq, tk):
    i = pl.program_id(1)
    s = k_ref.shape[1]
    dh = HEAD_DIM
    heads = q_ref.shape[2] // dh
    pairs = heads // 2

    @pl.when(i == 0)
    def _():
        def transpose_chunk(c, carry):
            c0 = pl.multiple_of(c * TR_CHUNK, TR_CHUNK)
            vt_ref[:, pl.ds(c0, TR_CHUNK)] = v_ref[0, pl.ds(c0, TR_CHUNK), :].astype(F32).T.astype(BF16)
            return carry
        lax.fori_loop(0, s // TR_CHUNK, transpose_chunk, 0)

    lane = lax.broadcasted_iota(I32, (tq, LANES), 1)
    for p in range(pairs):
        qp = q_ref[0, :, p * LANES:(p + 1) * LANES].astype(F32) * (QK_SCALE * LOG2E)
        qm_ref[2 * p] = jnp.where(lane < dh, qp, 0.0).astype(BF16)
        qm_ref[2 * p + 1] = jnp.where(lane >= dh, qp, 0.0).astype(BF16)
    acc_ref[...] = jnp.zeros_like(acc_ref)
    a_ref[...] = jnp.zeros_like(a_ref)
    rows = lax.broadcasted_iota(I32, (tk, tq), 0)
    tcol = i * tq + lax.broadcasted_iota(I32, (tk, tq), 1)
    tri = lax.broadcasted_iota(I32, (tk, tk), 1) > lax.broadcasted_iota(I32, (tk, tk), 0)
    tri = jnp.where(tri, 1.0, 0.0).astype(BF16)
    tri2 = jnp.concatenate([tri, tri], axis=1)

    def body(carry):
        j, _ = carry
        k0 = pl.multiple_of(j * tk, tk)
        mask = (k0 + rows) < tcol
        hs = range(heads)
        kbs = [k_ref[0, pl.ds(k0, tk), p * LANES:(p + 1) * LANES] for p in range(pairs)]
        zs = [lax.dot_general(kbs[h // 2], qm_ref[h], _NT, preferred_element_type=F32) for h in hs]
        sps = [jnp.log2(1.0 + jnp.exp2(-jnp.abs(z))) for z in zs]
        log_betas = [jnp.minimum(z, 0.0) - sp for z, sp in zip(zs, sps)]
        l1s = [jnp.where(mask, -jnp.maximum(z, 0.0) - sp, 0.0) for z, sp in zip(zs, sps)]
        his = [l1.astype(BF16) for l1 in l1s]
        los = [(l1 - hi.astype(F32)).astype(BF16) for l1, hi in zip(l1s, his)]
        es = [jnp.dot(tri2, jnp.concatenate([hi, lo], axis=0), preferred_element_type=F32)
              for hi, lo in zip(his, los)]
        a_olds = [a_ref[h] for h in hs]
        ws = [jnp.where(mask, jnp.exp2(lb + a + e), 0.0).astype(BF16) for lb, a, e in zip(log_betas, a_olds, es)]
        amax = None
        for h in hs:
            acc_ref[h] += jnp.dot(vt_ref[h * dh:(h + 1) * dh, pl.ds(k0, tk)], ws[h], preferred_element_type=F32)
            a_new = a_olds[h] + jnp.sum(l1s[h], axis=0, keepdims=True)
            a_ref[h] = a_new
            amax = a_new if amax is None else jnp.maximum(amax, a_new)
        return j - 1, jnp.max(amax) > EXP2_UNDERFLOW

    j0 = ((i + 1) * tq - 1) // tk
    lax.while_loop(lambda c: jnp.logical_and(c[0] >= 0, c[1]), body, (j0, True))
    for p in range(pairs):
        both = jnp.concatenate([acc_ref[2 * p], acc_ref[2 * p + 1]], axis=0)
        o_ref[0, :, p * LANES:(p + 1) * LANES] = both.T.astype(o_ref.dtype)


def _sb_attention(proj, tq=256, tk=128):
    b, s, _ = proj.shape
    w = SB_WIDTH
    return pl.pallas_call(
        functools.partial(_sb_kernel, tq=tq, tk=tk),
        grid=(b, s // tq),
        in_specs=[pl.BlockSpec((1, tq, w), lambda bi, i: (bi, i, 0)),
                  pl.BlockSpec((1, s, w), lambda bi, i: (bi, 0, 1)),
                  pl.BlockSpec((1, s, w), lambda bi, i: (bi, 0, 2))],
        out_specs=pl.BlockSpec((1, tq, w), lambda bi, i: (bi, i, 0)),
        out_shape=jax.ShapeDtypeStruct((b, s, w), BF16),
        scratch_shapes=[pltpu.VMEM((w, s), BF16),
                        pltpu.VMEM((SB_HEADS, tq, LANES), BF16),
                        pltpu.VMEM((SB_HEADS, HEAD_DIM, tq), F32),
                        pltpu.VMEM((SB_HEADS, 1, tq), F32)],
        compiler_params=_params("parallel", "arbitrary"),
        name="sb_attention",
    )(proj, proj, proj)


def _ssm_kmat_kernel(l_ref, b_ref, o_ref):
    for g in range(l_ref.shape[0]):
        o_ref[g] = jnp.dot(l_ref[g], b_ref[g], precision=lax.Precision.HIGHEST,
                           preferred_element_type=F32)


def _ssm_kmat(lcat, bcat, gb=8):
    g, r, n2 = lcat.shape
    c = bcat.shape[2]
    return pl.pallas_call(
        _ssm_kmat_kernel,
        grid=(g // gb,),
        in_specs=[pl.BlockSpec((gb, r, n2), lambda i: (i, 0, 0)),
                  pl.BlockSpec((gb, n2, c), lambda i: (i, 0, 0))],
        out_specs=pl.BlockSpec((gb, r, c), lambda i: (i, 0, 0)),
        out_shape=jax.ShapeDtypeStruct((g, r, c), F32),
        compiler_params=_params("parallel"),
        name="ssm_kmat",
    )(lcat, bcat)


def _cmul(ar, ai, xr, xi):
    return ar * xr - ai * xi, ar * xi + ai * xr


def _ssm_kernel(u_ref, wk_ref, qr_ref, qi_ref, rr_ref, ri_ref, al_ref, aseg_ref, d_ref, y_ref,
                uf_ref, uc_ref, xr_ref, xi_ref, *, nch, nseg):
    L = SSM_CHUNK
    seg_rows = nch * L
    rows = nch * nseg
    uf_ref[...] = u_ref[0].astype(F32)
    for c in range(nch):
        for s in range(L):
            uc_ref[c * nseg:(c + 1) * nseg, s * LANES:(s + 1) * LANES] = \
                uf_ref[pl.ds(c * L + s, nseg, stride=seg_rows), :]
    ucb = uc_ref[...].astype(BF16)
    xr_ref[...] = jnp.dot(ucb, qr_ref[0], preferred_element_type=F32)
    xi_ref[...] = jnp.dot(ucb, qi_ref[0], preferred_element_type=F32)
    alr, ali = al_ref[0, 0:1, :], al_ref[0, 1:2, :]

    def scan_body(c, carry):
        sr, si = carry
        r0 = pl.multiple_of(c * nseg, nseg)
        inr = xr_ref[pl.ds(r0, nseg), :]
        ini = xi_ref[pl.ds(r0, nseg), :]
        xr_ref[pl.ds(r0, nseg), :] = sr
        xi_ref[pl.ds(r0, nseg), :] = si
        pr, pi = _cmul(alr, ali, sr, si)
        return pr + inr, pi + ini

    zero = jnp.zeros((nseg, xr_ref.shape[1]), F32)
    er, ei = lax.fori_loop(0, nch, scan_body, (zero, zero))

    row = lax.broadcasted_iota(I32, zero.shape, 0)
    cr, ci = zero, zero
    for _ in range(nseg - 1):
        pr, pi = _cmul(aseg_ref[0, 0:1, :], aseg_ref[0, 1:2, :], cr, ci)
        cr = jnp.where(row >= 1, pltpu.roll(er + pr, 1, 0), 0.0)
        ci = jnp.where(row >= 1, pltpu.roll(ei + pi, 1, 0), 0.0)

    def corr_body(c, carry):
        cr, ci = carry
        r0 = pl.multiple_of(c * nseg, nseg)
        xr_ref[pl.ds(r0, nseg), :] += cr
        xi_ref[pl.ds(r0, nseg), :] += ci
        return _cmul(alr, ali, cr, ci)

    lax.fori_loop(0, nch, corr_body, (cr, ci))

    yc = jnp.dot(xr_ref[...].astype(BF16), rr_ref[0], preferred_element_type=F32)
    yc += jnp.dot(xi_ref[...].astype(BF16), ri_ref[0], preferred_element_type=F32)
    for t in range(L):
        lanes = slice(t * LANES, (t + 1) * LANES)
        y_t = jnp.dot(ucb[:, :(t + 1) * LANES], wk_ref[0, (L - 1 - t) * LANES:, :], preferred_element_type=F32)
        y_t += yc[:, lanes] + uc_ref[:, lanes] * d_ref[0]
        for c in range(nch):
            y_ref[0, pl.ds(c * L + t, nseg, stride=seg_rows), :] = y_t[c * nseg:(c + 1) * nseg]


def _ssm_scan(proj, wk, qr, qi, rr, ri, al, aseg, d_t, *, nch, nseg):
    b, s, n = proj.shape
    tiles = SSM_WIDTH // LANES
    first = (n - SSM_WIDTH) // LANES
    rows = nch * nseg
    w = SSM_CHUNK * LANES
    ns = qr.shape[2]
    per_tile = lambda r, c: pl.BlockSpec((1, r, c), lambda j, bi: (j, 0, 0))
    return pl.pallas_call(
        functools.partial(_ssm_kernel, nch=nch, nseg=nseg),
        grid=(tiles, b),
        in_specs=[pl.BlockSpec((1, s, LANES), lambda j, bi: (bi, 0, first + j)),
                  per_tile(w, LANES), per_tile(w, ns), per_tile(w, ns), per_tile(ns, w), per_tile(ns, w),
                  per_tile(2, ns), per_tile(2, ns), per_tile(1, LANES)],
        out_specs=pl.BlockSpec((1, s, LANES), lambda j, bi: (bi, 0, j)),
        out_shape=jax.ShapeDtypeStruct((b, s, SSM_WIDTH), F32),
        scratch_shapes=[pltpu.VMEM((s, LANES), F32), pltpu.VMEM((rows, w), F32),
                        pltpu.VMEM((rows, ns), F32), pltpu.VMEM((rows, ns), F32)],
        compiler_params=_params("parallel", "parallel"),
        name="ssm_scan",
    )(proj, wk, qr, qi, rr, ri, al, aseg, d_t)


def _ssm_operators(log_dt, lam_re, lam_im, b_re, b_im, c_re, c_im, d, nch):
    L, C, N = SSM_CHUNK, SSM_GROUP, SSM_STATE
    G = log_dt.shape[0]
    dt = jnp.exp(log_dt.astype(F32))[:, None]
    lr = lam_re.astype(F32)
    li = lam_im.astype(F32)

    def apow(tau):
        tau = jnp.asarray(tau, F32)[..., None, None]
        mag = jnp.exp(tau * (lr * dt))
        ang = tau * (li * dt)
        return mag * jnp.cos(ang), mag * jnp.sin(ang)

    pr, pi = apow(np.arange(L + 1))
    den = lr * lr + li * li
    nr = pr[1] - 1.0
    coef_r = ((nr * lr + pi[1] * li) / den)[..., None]
    coef_i = ((pi[1] * lr - nr * li) / den)[..., None]
    br = b_re.astype(F32)
    bi = b_im.astype(F32)
    bbar_r = coef_r * br - coef_i * bi
    bbar_i = coef_r * bi + coef_i * br
    cr = c_re.astype(F32)
    ci = c_im.astype(F32)
    car = cr[None] * pr[:, :, None, :] - ci[None] * pi[:, :, None, :]
    cai = cr[None] * pi[:, :, None, :] + ci[None] * pr[:, :, None, :]
    lcat = jnp.concatenate([car[:L], -cai[:L]], axis=-1)
    lcat = lcat.transpose(1, 0, 2, 3).reshape(G, L * C, 2 * N)
    bcat = jnp.concatenate([bbar_r, bbar_i], axis=1)
    kmat = _ssm_kmat(lcat, bcat).reshape(G, L, C, C)
    gt = LANES // C
    tiles = G // gt
    same = np.eye(gt, dtype=bool)
    k5 = kmat.reshape(tiles, gt, L, C, C)[:, :, ::-1].transpose(0, 2, 1, 4, 3).astype(BF16)
    wk = jnp.where(same[None, None, :, None, :, None], k5[:, :, :, :, None, :], 0)
    wk = wk.reshape(tiles, L * LANES, LANES)
    prq = pr[L - 1 - np.arange(L)]
    piq = pi[L - 1 - np.arange(L)]
    qr = prq[..., None] * bbar_r[None] - piq[..., None] * bbar_i[None]
    qi = prq[..., None] * bbar_i[None] + piq[..., None] * bbar_r[None]

    def tile_q(q):
        q = q.reshape(L, tiles, gt, N, C).transpose(1, 0, 2, 4, 3).astype(BF16)
        q = jnp.where(same[None, None, :, None, :, None], q[:, :, :, :, None, :], 0)
        return q.reshape(tiles, L * LANES, gt * N)

    qr, qi = tile_q(qr), tile_q(qi)

    def tile_r(r):
        r = r.reshape(L, tiles, gt, C, N).transpose(1, 2, 4, 0, 3).astype(BF16)
        r = jnp.where(same[None, :, None, None, :, None], r[:, :, :, :, None, :], 0)
        return r.reshape(tiles, gt * N, L * LANES)

    rr, ri = tile_r(car[1:]), tile_r(-cai[1:])
    sr, si = apow(np.asarray([L * nch]))
    al = jnp.stack([pr[L], pi[L]], axis=1).reshape(tiles, gt, 2, N).transpose(0, 2, 1, 3).reshape(tiles, 2, gt * N)
    aseg = jnp.stack([sr[0], si[0]], axis=1).reshape(tiles, gt, 2, N).transpose(0, 2, 1, 3).reshape(tiles, 2, gt * N)
    d_t = d.astype(F32).reshape(tiles, 1, LANES)
    return wk, qr, qi, rr, ri, al, aseg, d_t


def _mlp_ln(x, w1_ref, w2_ref, g_ref, b_ref):
    xb = x.astype(BF16)
    acc = jnp.zeros(x.shape, F32)
    for f in range(0, w1_ref.shape[1], MLP_FCHUNK):
        h = jnp.dot(xb, w1_ref[:, f:f + MLP_FCHUNK], preferred_element_type=F32)
        h = jnp.maximum(h, 0.0)
        acc += jnp.dot((h * h).astype(BF16), w2_ref[f:f + MLP_FCHUNK, :], preferred_element_type=F32)
    return _layer_norm(ALPHA * x + acc, g_ref[...], b_ref[...])


def _tail_even_kernel(a_ref, y_ref, x_ref, wglu_ref, bglu_ref, wo_ref, g1_ref, b1_ref,
                      w1_ref, w2_ref, g2_ref, b2_ref, o_ref):
    y = jax.nn.gelu(y_ref[...], approximate=True)
    gate = jnp.dot(y.astype(BF16), wglu_ref[...], preferred_element_type=F32) + bglu_ref[...]
    y = y * (1.0 / (1.0 + jnp.exp(-gate)))
    h = jnp.dot(a_ref[...], wo_ref[:SB_WIDTH, :], preferred_element_type=F32)
    h += jnp.dot(y.astype(BF16), wo_ref[SB_WIDTH:, :], preferred_element_type=F32)
    x1 = _layer_norm(ALPHA * x_ref[...] + h, g1_ref[...], b1_ref[...])
    o_ref[...] = _mlp_ln(x1, w1_ref, w2_ref, g2_ref, b2_ref)


def _tail_odd_kernel(o_in_ref, x_ref, wo_ref, g1_ref, b1_ref, w1_ref, w2_ref, g2_ref, b2_ref, o_ref):
    h = jnp.dot(o_in_ref[...], wo_ref[...], preferred_element_type=F32)
    x1 = _layer_norm(ALPHA * x_ref[...] + h, g1_ref[...], b1_ref[...])
    o_ref[...] = _mlp_ln(x1, w1_ref, w2_ref, g2_ref, b2_ref)


def _layer_tail(body, name, rows, consts, tm=512):
    m, d = rows[-1].shape
    row = lambda a: pl.BlockSpec((tm, a.shape[1]), lambda i: (i, 0))
    once = lambda a: pl.BlockSpec(a.shape, lambda i: (0, 0), pipeline_mode=pl.Buffered(1))
    return pl.pallas_call(
        body,
        grid=(m // tm,),
        in_specs=[row(a) for a in rows] + [once(a) for a in consts],
        out_specs=pl.BlockSpec((tm, d), lambda i: (i, 0)),
        out_shape=jax.ShapeDtypeStruct((m, d), F32),
        compiler_params=_params("parallel"),
        name=name,
    )(*rows, *consts)


def _bf16_split3(x):
    x = np.asarray(x, np.float32)
    hi = x.astype(BF16).astype(np.float32)
    mid = (x - hi).astype(BF16).astype(np.float32)
    lo = (x - hi - mid).astype(BF16).astype(np.float32)
    return hi, mid, lo


def _alibi_columns(s):
    slopes = (2.0 ** (-8.0 * np.arange(1, DSA_HEADS + 1) / DSA_HEADS)).astype(np.float32)
    slopes = (slopes.astype(np.float64) * LOG2E).astype(np.float32)
    parts = _bf16_split3(slopes)
    assert np.all(parts[0] + parts[1] + parts[2] == slopes)
    q_cols = np.zeros((DSA_HEADS, 1, HEAD_DIM), np.float32)
    q_cols[:, 0, :6] = np.stack(parts * 2, axis=-1)
    pos = np.arange(s)
    k_cols = np.zeros((s, HEAD_DIM), np.float32)
    k_cols[:, 0:3] = (pos // 64 * 64)[:, None]
    k_cols[:, 3:6] = (pos % 64)[:, None]
    return jnp.asarray(q_cols, BF16), jnp.asarray(k_cols, BF16)


def _key_to_float(key):
    bits = key ^ ((key >> 31) & jnp.int32(0x7FFFFFFF))
    return lax.bitcast_convert_type(bits, F32)


def _dsa_kernel(q_ref, k_ref, v_ref, qi_ref, kiw_ref, kiwq_ref, qcols_ref, kcols_ref, o_ref,
                kext_ref, vt_ref, qie_ref, sc_ref, m_ref, acc_ref, qs_ref, bias_ref, xs_ref,
                *, tq, tkb, tk, topk):
    i = pl.program_id(1)
    t0 = i * tq
    s = k_ref.shape[1]
    dh = HEAD_DIM
    kv_heads = k_ref.shape[2] // dh
    n_heads = q_ref.shape[2] // dh
    rep = n_heads // kv_heads
    n_idx_heads = qi_ref.shape[2] // IDX_DIM

    @pl.when(i == 0)
    def _():
        def prep_chunk(c, carry):
            c0 = pl.multiple_of(c * TR_CHUNK, TR_CHUNK)
            kblk = k_ref[0, pl.ds(c0, TR_CHUNK), :].astype(F32)
            vblk_t = v_ref[0, pl.ds(c0, TR_CHUNK), :].astype(F32).T
            for g in range(kv_heads):
                kext_ref[g, pl.ds(c0, TR_CHUNK), 0:dh] = kblk[:, g * dh:(g + 1) * dh].astype(BF16)
                kext_ref[g, pl.ds(c0, TR_CHUNK), dh:2 * dh] = kcols_ref[pl.ds(c0, TR_CHUNK), :]
                vt_ref[g, 0:dh, pl.ds(c0, TR_CHUNK)] = vblk_t[g * dh:(g + 1) * dh].astype(BF16)
            return carry
        lax.fori_loop(0, s // TR_CHUNK, prep_chunk, 0)
        ones_row = lax.broadcasted_iota(I32, (BF16_SUBLANES, s), 0) == 0
        for g in range(kv_heads):
            vt_ref[g, dh:dh + BF16_SUBLANES, :] = jnp.where(ones_row, 1.0, 0.0).astype(BF16)

    qf = q_ref[0].astype(F32) * (QK_SCALE * LOG2E)
    for h in range(n_heads):
        qs_ref[h, :, 0:dh] = qf[:, h * dh:(h + 1) * dh].astype(BF16)
        qs_ref[h, :, dh:2 * dh] = jnp.broadcast_to(qcols_ref[h], (tq, dh))
    qif = qi_ref[0].astype(F32)
    lane = lax.broadcasted_iota(I32, (tq, LANES), 1)
    per_tile = LANES // IDX_DIM
    for h in range(n_idx_heads):
        tile = qif[:, (h // per_tile) * LANES:(h // per_tile + 1) * LANES]
        shift = (h % per_tile) * IDX_DIM
        if shift:
            tile = pltpu.roll(tile, LANES - shift, 1)
        qie_ref[h] = jnp.where(lane < IDX_DIM, tile, 0.0).astype(BF16)
    wt = kiwq_ref[0].astype(F32).T[IDX_DIM:IDX_DIM + n_idx_heads] * IDX_SCALE

    nsb = (t0 + tq + tkb - 1) // tkb
    srow = lax.broadcasted_iota(I32, (tkb, tq), 0)
    tcol = t0 + lax.broadcasted_iota(I32, (tkb, tq), 1)

    def score_body(jb, carry):
        s0 = pl.multiple_of(jb * tkb, tkb)
        kib = kiw_ref[0, pl.ds(s0, tkb), :]
        sc = jnp.zeros((tkb, tq), F32)
        for h in range(n_idx_heads):
            z = lax.dot_general(kib, qie_ref[h], _NT, preferred_element_type=F32)
            sc += wt[h:h + 1, :] * jnp.maximum(z, 0.0)
        sc_ref[pl.ds(s0, tkb), :] = jnp.where(s0 + srow <= tcol, sc, -jnp.inf)
        return carry

    lax.fori_loop(0, nsb, score_body, 0)

    def count(pred):
        def body(jb, acc):
            s0 = pl.multiple_of(jb * tkb, tkb)
            ind = jnp.where(pred(sc_ref[pl.ds(s0, tkb), :], s0 + srow), 1.0, 0.0)
            return acc + jnp.sum(ind.reshape(tkb // COUNT_ROWS, COUNT_ROWS, tq), axis=0)
        acc = lax.fori_loop(0, nsb, body, jnp.zeros((COUNT_ROWS, tq), F32))
        return jnp.sum(acc, axis=0, keepdims=True)

    kf = float(topk)
    c0 = count(lambda sc, sidx: sc >= 0.0)
    ok0 = c0 >= kf
    thr_key0 = jnp.where(ok0, jnp.int32(0), jnp.int32(INT_MIN))
    cnt0 = jnp.where(ok0, c0, 0.0)

    def thr_cond(carry):
        it, _, cnt = carry
        return jnp.logical_and(it < 31, jnp.max(jnp.abs(cnt - kf)) > 0.0)

    def bit_body(it, carry):
        thr_key, cnt = carry
        cand = thr_key | (jnp.int32(1) << (30 - it))
        cand_f = _key_to_float(cand)
        c = count(lambda sc, sidx: sc >= cand_f)
        ok = c >= kf
        return jnp.where(ok, cand, thr_key), jnp.where(ok, c, cnt)

    def thr_body(carry):
        it, thr_key, cnt = carry
        nxt = jnp.minimum(it + BITS_PER_CHECK, 31)
        thr_key, cnt = lax.fori_loop(it, nxt, bit_body, (thr_key, cnt))
        return nxt, thr_key, cnt

    _, thr_key, n_ge = lax.while_loop(thr_cond, thr_body, (jnp.int32(0), thr_key0, cnt0))
    thr = jnp.where(thr_key == INT_MIN, -jnp.inf, _key_to_float(thr_key))
    has_ties = jnp.max(n_ge) > kf
    tie_passes = jnp.where(has_ties, 1, 0)
    n_gt = lax.fori_loop(0, tie_passes, lambda it, c: count(lambda sc, sidx: sc > thr), jnp.zeros((1, tq), F32))
    need = kf - n_gt

    def cut_body(it, cut):
        cand = cut | (jnp.int32(1) << (12 - it))
        c = count(lambda sc, sidx: jnp.logical_and(sc == thr, sidx < cand))
        return jnp.where(c <= need, cand, cut)

    cut = lax.fori_loop(0, jnp.where(has_ties, 13, 0), cut_body, jnp.zeros((1, tq), I32))
    cut = jnp.where(has_ties, cut, jnp.int32(2 ** 30))

    nkb = (t0 + tq + tk - 1) // tk
    srow_k = lax.broadcasted_iota(I32, (tk, tq), 0)
    tcol_k = t0 + lax.broadcasted_iota(I32, (tk, tq), 1)

    def set_block_bias(jb):
        s0 = pl.multiple_of(jb * tk, tk)
        sc = sc_ref[pl.ds(s0, tk), :]
        sidx = s0 + srow_k
        sel = jnp.logical_or(sc > thr, jnp.logical_and(sc == thr, sidx < cut))
        sel = jnp.logical_and(sel, sidx <= tcol_k)
        bias_ref[...] = jnp.where(sel, 0.0, NEG_BIG)
        return s0

    def logits(h, kb):
        return lax.dot_general(kb, qs_ref[h], _NT, preferred_element_type=F32)

    m_ref[...] = jnp.full(m_ref.shape, NEG_BIG, F32)
    acc_ref[...] = jnp.zeros_like(acc_ref)

    def attn_body(jb, carry):
        s0 = set_block_bias(jb)
        maxima = []
        for g in range(kv_heads):
            kb = kext_ref[g, pl.ds(s0, tk), :]
            for r in range(rep):
                h = g * rep + r
                x = logits(h, kb) + bias_ref[...]
                xs_ref[h] = x
                maxima.append(jnp.max(x, axis=0, keepdims=True))
        for g in range(kv_heads):
            vt = vt_ref[g, :, pl.ds(s0, tk)]
            for r in range(rep):
                h = g * rep + r
                m_prev = m_ref[h]
                m_new = jnp.maximum(m_prev, maxima[h])
                p = jnp.exp2(xs_ref[h] - m_new).astype(BF16)
                acc_ref[h] = jnp.exp2(m_prev - m_new) * acc_ref[h] + jnp.dot(
                    vt, p, preferred_element_type=F32)
                m_ref[h] = m_new
        return carry

    lax.fori_loop(0, nkb, attn_body, 0)
    for p in range(n_heads // 2):
        outs = []
        for h in (2 * p, 2 * p + 1):
            acc = acc_ref[h]
            outs.append(acc[:dh] / acc[dh:dh + 1])
        o_ref[0, :, p * LANES:(p + 1) * LANES] = jnp.concatenate(outs, axis=0).T.astype(o_ref.dtype)


def _dsa_attention(proj, topk, tq=256, tkb=512, tk=256):
    b, s, n = proj.shape
    dh = HEAD_DIM
    qw = DSA_HEADS * dh
    kvw = DSA_KV_HEADS * dh
    iw = IDX_HEADS * IDX_DIM
    assert qw % kvw == 0 and (qw + 2 * kvw) % iw == 0 and (qw + 2 * kvw + iw) % LANES == 0
    assert IDX_DIM + IDX_HEADS <= LANES and n == qw + 2 * kvw + iw + LANES
    q_cols, k_cols = _alibi_columns(s)
    tkb = min(tkb, s)
    tk = min(tk, s)
    kiw_block = (qw + 2 * kvw + iw) // LANES
    dv = dh + BF16_SUBLANES
    return pl.pallas_call(
        functools.partial(_dsa_kernel, tq=tq, tkb=tkb, tk=tk, topk=topk),
        grid=(b, s // tq),
        in_specs=[pl.BlockSpec((1, tq, qw), lambda bi, i: (bi, i, 0)),
                  pl.BlockSpec((1, s, kvw), lambda bi, i: (bi, 0, qw // kvw)),
                  pl.BlockSpec((1, s, kvw), lambda bi, i: (bi, 0, qw // kvw + 1)),
                  pl.BlockSpec((1, tq, iw), lambda bi, i: (bi, i, (qw + 2 * kvw) // iw)),
                  pl.BlockSpec((1, s, LANES), lambda bi, i: (bi, 0, kiw_block)),
                  pl.BlockSpec((1, tq, LANES), lambda bi, i: (bi, i, kiw_block)),
                  pl.BlockSpec((DSA_HEADS, 1, dh), lambda bi, i: (0, 0, 0)),
                  pl.BlockSpec((s, dh), lambda bi, i: (0, 0))],
        out_specs=pl.BlockSpec((1, tq, qw), lambda bi, i: (bi, i, 0)),
        out_shape=jax.ShapeDtypeStruct((b, s, qw), BF16),
        scratch_shapes=[pltpu.VMEM((DSA_KV_HEADS, s, 2 * dh), BF16),
                        pltpu.VMEM((DSA_KV_HEADS, dv, s), BF16),
                        pltpu.VMEM((IDX_HEADS, tq, LANES), BF16),
                        pltpu.VMEM((s, tq), F32),
                        pltpu.VMEM((DSA_HEADS, 1, tq), F32),
                        pltpu.VMEM((DSA_HEADS, dv, tq), F32),
                        pltpu.VMEM((DSA_HEADS, tq, 2 * dh), BF16),
                        pltpu.VMEM((tk, tq), F32),
                        pltpu.VMEM((DSA_HEADS, tk, tq), F32)],
        compiler_params=_params("parallel", "arbitrary"),
        name="dsa_attention",
    )(proj, proj, proj, proj, proj, proj, q_cols, k_cols)


def _row(v):
    return v.astype(F32)[None]


def _even_layer(x2d, bsz, s, w_in, ssm_ops, w_glu, b_glu, w_out, g_mix, b_mix, mlp):
    proj = _proj(x2d, w_in.astype(BF16)).reshape(bsz, s, -1)
    a_out = _sb_attention(proj).reshape(bsz * s, SB_WIDTH)
    y = _ssm_scan(proj, *ssm_ops, nch=s // (SSM_CHUNK * SUBLANES), nseg=SUBLANES).reshape(bsz * s, SSM_WIDTH)
    consts = (w_glu.astype(BF16), _row(b_glu), w_out.astype(BF16), _row(g_mix), _row(b_mix)) + mlp
    return _layer_tail(_tail_even_kernel, "tail_even", (a_out, y, x2d), consts)


def _odd_layer(x2d, bsz, s, w_in, w_out, g_mix, b_mix, mlp):
    proj = _proj(x2d, w_in.astype(BF16)).reshape(bsz, s, -1)
    o = _dsa_attention(proj, min(TOPK_MAX, s // 4)).reshape(bsz * s, DSA_HEADS * HEAD_DIM)
    consts = (w_out.astype(BF16), _row(g_mix), _row(b_mix)) + mlp
    return _layer_tail(_tail_odd_kernel, "tail_odd", (o, x2d), consts)


def kernel(x, sb_ssm_w_in, ssm_log_dt, ssm_lam_re, ssm_lam_im, ssm_b_re, ssm_b_im, ssm_c_re, ssm_c_im, ssm_d,
           ssm_w_glu, ssm_b_glu, sb_ssm_w_out, dsa_w_in, dsa_w_out, ln_mix_g, ln_mix_b, ln_ffn_g, ln_ffn_b,
           mlp_w1, mlp_w2):
    bsz, s, d = x.shape
    x2d = x.reshape(bsz * s, d).astype(F32)
    depth = mlp_w1.shape[0]
    stacked = [p.reshape((-1,) + p.shape[2:]) for p in
               (ssm_log_dt, ssm_lam_re, ssm_lam_im, ssm_b_re, ssm_b_im, ssm_c_re, ssm_c_im, ssm_d)]
    ssm_ops = _ssm_operators(*stacked, nch=s // (SSM_CHUNK * SUBLANES))
    tiles = SSM_WIDTH // LANES
    for i in range(depth):
        j = i // 2
        mlp = (mlp_w1[i].astype(BF16), mlp_w2[i].astype(BF16), _row(ln_ffn_g[i]), _row(ln_ffn_b[i]))
        if i % 2 == 0:
            layer_ops = tuple(op[j * tiles:(j + 1) * tiles] for op in ssm_ops)
            x2d = _even_layer(x2d, bsz, s, sb_ssm_w_in[j], layer_ops, ssm_w_glu[j], ssm_b_glu[j],
                              sb_ssm_w_out[j], ln_mix_g[i], ln_mix_b[i], mlp)
        else:
            n_in = dsa_w_in.shape[2]
            pad = (-n_in) % LANES
            w_in = jnp.pad(dsa_w_in[j], ((0, 0), (0, pad)))
            x2d = _odd_layer(x2d, bsz, s, w_in, dsa_w_out[j], ln_mix_g[i], ln_mix_b[i], mlp)
    return x2d.reshape(bsz, s, d).astype(x.dtype)
```

```python
import functools
import math

import jax
import jax.numpy as jnp
import numpy as np
from jax import lax
from jax.experimental import pallas as pl
from jax.experimental.pallas import tpu as pltpu

F32 = jnp.float32
BF16 = jnp.bfloat16
I32 = jnp.int32

D_MODEL = 1024
HEAD_DIM = 64
SB_WIDTH = 512
SB_HEADS = SB_WIDTH // HEAD_DIM
SSM_WIDTH = 512
SSM_GROUP = 16
SSM_GROUPS = SSM_WIDTH // SSM_GROUP
SSM_STATE = 64
DSA_HEADS = 16
DSA_KV_HEADS = 4
DSA_REP = DSA_HEADS // DSA_KV_HEADS
IDX_HEADS = 8
IDX_DIM = 32
TOPK_MAX = 256
D_FF = 4 * D_MODEL
DEPTH = 4
ALPHA = (2 * DEPTH) ** 0.25
LN_EPS = 1e-5
IDX_SCALE = (IDX_DIM ** -0.5) * (IDX_HEADS ** -0.5)
QK_SCALE = HEAD_DIM ** -0.5

SSM_CHUNK = 16
SUBLANES = 8
BF16_SUBLANES = 16
LANES = 128
VMEM_LIMIT = 56 * 1024 * 1024
EXP2_UNDERFLOW = -150.0
NEG_BIG = -1e30
INT_MIN = -2 ** 31
COUNT_ROWS = 64
TR_CHUNK = 256
LOG2E = math.log2(math.e)
MLP_FCHUNK = 1024
BITS_PER_CHECK = 8

_NT = (((1,), (1,)), ((), ()))


def _params(*sem):
    return pltpu.CompilerParams(dimension_semantics=sem, vmem_limit_bytes=VMEM_LIMIT)


def _layer_norm(r, g, b):
    mu = jnp.mean(r, axis=-1, keepdims=True)
    c = r - mu
    var = jnp.mean(c * c, axis=-1, keepdims=True)
    return c * lax.rsqrt(var + LN_EPS) * g + b


def _proj_kernel(x_ref, w_ref, o_ref):
    o_ref[...] = jnp.dot(x_ref[...].astype(BF16), w_ref[...],
                         preferred_element_type=F32).astype(o_ref.dtype)


def _proj(x2d, w, tm=512):
    m, k = x2d.shape
    n = w.shape[1]
    return pl.pallas_call(
        _proj_kernel,
        grid=(m // tm,),
        in_specs=[pl.BlockSpec((tm, k), lambda i: (i, 0)),
                  pl.BlockSpec((k, n), lambda i: (0, 0))],
        out_specs=pl.BlockSpec((tm, n), lambda i: (i, 0)),
        out_shape=jax.ShapeDtypeStruct((m, n), BF16),
        compiler_params=_params("parallel"),
        name="proj",
    )(x2d, w)


def _sb_kernel(q_ref, k_ref, v_ref, o_ref, vt_ref, qm_ref, acc_ref, a_ref, *, tq, tk):
    i = pl.program_id(1)
    s = k_ref.shape[1]
    dh = HEAD_DIM
    heads = q_ref.shape[2] // dh
    pairs = heads // 2

    @pl.when(i == 0)
    def _():
        def transpose_chunk(c, carry):
            c0 = pl.multiple_of(c * TR_CHUNK, TR_CHUNK)
            vt_ref[:, pl.ds(c0, TR_CHUNK)] = v_ref[0, pl.ds(c0, TR_CHUNK), :].astype(F32).T.astype(BF16)
            return carry
        lax.fori_loop(0, s // TR_CHUNK, transpose_chunk, 0)

    lane = lax.broadcasted_iota(I32, (tq, LANES), 1)
    for p in range(pairs):
        qp = q_ref[0, :, p * LANES:(p + 1) * LANES].astype(F32) * (QK_SCALE * LOG2E)
        qm_ref[2 * p] = jnp.where(lane < dh, qp, 0.0).astype(BF16)
        qm_ref[2 * p + 1] = jnp.where(lane >= dh, qp, 0.0).astype(BF16)
    acc_ref[...] = jnp.zeros_like(acc_ref)
    a_ref[...] = jnp.zeros_like(a_ref)
    rows = lax.broadcasted_iota(I32, (tk, tq), 0)
    tcol = i * tq + lax.broadcasted_iota(I32, (tk, tq), 1)
    tri = lax.broadcasted_iota(I32, (tk, tk), 1) > lax.broadcasted_iota(I32, (tk, tk), 0)
    tri = jnp.where(tri, 1.0, 0.0).astype(BF16)
    tri2 = jnp.concatenate([tri, tri], axis=1)

    def body(carry):
        j, _ = carry
        k0 = pl.multiple_of(j * tk, tk)
        mask = (k0 + rows) < tcol
        hs = range(heads)
        kbs = [k_ref[0, pl.ds(k0, tk), p * LANES:(p + 1) * LANES] for p in range(pairs)]
        zs = [lax.dot_general(kbs[h // 2], qm_ref[h], _NT, preferred_element_type=F32) for h in hs]
        sps = [jnp.log2(1.0 + jnp.exp2(-jnp.abs(z))) for z in zs]
        log_betas = [jnp.minimum(z, 0.0) - sp for z, sp in zip(zs, sps)]
        l1s = [jnp.where(mask, -jnp.maximum(z, 0.0) - sp, 0.0) for z, sp in zip(zs, sps)]
        his = [l1.astype(BF16) for l1 in l1s]
        los = [(l1 - hi.astype(F32)).astype(BF16) for l1, hi in zip(l1s, his)]
        es = [jnp.dot(tri2, jnp.concatenate([hi, lo], axis=0), preferred_element_type=F32)
              for hi, lo in zip(his, los)]
        a_olds = [a_ref[h] for h in hs]
        ws = [jnp.where(mask, jnp.exp2(lb + a + e), 0.0).astype(BF16) for lb, a, e in zip(log_betas, a_olds, es)]
        amax = None
        for h in hs:
            acc_ref[h] += jnp.dot(vt_ref[h * dh:(h + 1) * dh, pl.ds(k0, tk)], ws[h], preferred_element_type=F32)
            a_new = a_olds[h] + jnp.sum(l1s[h], axis=0, keepdims=True)
            a_ref[h] = a_new
            amax = a_new if amax is None else jnp.maximum(amax, a_new)
        return j - 1, jnp.max(amax) > EXP2_UNDERFLOW

    j0 = ((i + 1) * tq - 1) // tk
    lax.while_loop(lambda c: jnp.logical_and(c[0] >= 0, c[1]), body, (j0, True))
    for p in range(pairs):
        both = jnp.concatenate([acc_ref[2 * p], acc_ref[2 * p + 1]], axis=0)
        o_ref[0, :, p * LANES:(p + 1) * LANES] = both.T.astype(o_ref.dtype)


def _sb_attention(proj, tq=256, tk=128):
    b, s, _ = proj.shape
    w = SB_WIDTH
    return pl.pallas_call(
        functools.partial(_sb_kernel, tq=tq, tk=tk),
        grid=(b, s // tq),
        in_specs=[pl.BlockSpec((1, tq, w), lambda bi, i: (bi, i, 0)),
                  pl.BlockSpec((1, s, w), lambda bi, i: (bi, 0, 1)),
                  pl.BlockSpec((1, s, w), lambda bi, i: (bi, 0, 2))],
        out_specs=pl.BlockSpec((1, tq, w), lambda bi, i: (bi, i, 0)),
        out_shape=jax.ShapeDtypeStruct((b, s, w), BF16),
        scratch_shapes=[pltpu.VMEM((w, s), BF16),
                        pltpu.VMEM((SB_HEADS, tq, LANES), BF16),
                        pltpu.VMEM((SB_HEADS, HEAD_DIM, tq), F32),
                        pltpu.VMEM((SB_HEADS, 1, tq), F32)],
        compiler_params=_params("parallel", "arbitrary"),
        name="sb_attention",
    )(proj, proj, proj)


def _ssm_kmat_kernel(l_ref, b_ref, o_ref):
    for g in range(l_ref.shape[0]):
        o_ref[g] = jnp.dot(l_ref[g], b_ref[g], precision=lax.Precision.HIGHEST,
                           preferred_element_type=F32)


def _ssm_kmat(lcat, bcat, gb=8):
    g, r, n2 = lcat.shape
    c = bcat.shape[2]
    return pl.pallas_call(
        _ssm_kmat_kernel,
        grid=(g // gb,),
        in_specs=[pl.BlockSpec((gb, r, n2), lambda i: (i, 0, 0)),
                  pl.BlockSpec((gb, n2, c), lambda i: (i, 0, 0))],
        out_specs=pl.BlockSpec((gb, r, c), lambda i: (i, 0, 0)),
        out_shape=jax.ShapeDtypeStruct((g, r, c), F32),
        compiler_params=_params("parallel"),
        name="ssm_kmat",
    )(lcat, bcat)


def _ssm_expand_kernel(e_ref, k_ref, qr_ref, qi_ref, rr_ref, ri_ref, wk_o, qr_o, qi_o, rr_o, ri_o):
    C, N, gt = SSM_GROUP, SSM_STATE, LANES // SSM_GROUP
    e = e_ref[...]

    def rows_out(x, lanes_per_group):
        y = jnp.dot(e, x, preferred_element_type=F32)
        row_group = (lax.broadcasted_iota(I32, y.shape, 0) // C) % gt
        lane_group = lax.broadcasted_iota(I32, y.shape, 1) // lanes_per_group
        return jnp.where(row_group == lane_group, y, 0.0).astype(BF16)

    def lanes_out(x):
        y = lax.dot_general(x, e, _NT, preferred_element_type=F32)
        row_group = lax.broadcasted_iota(I32, y.shape, 0) // N
        lane_group = (lax.broadcasted_iota(I32, y.shape, 1) // C) % gt
        return jnp.where(row_group == lane_group, y, 0.0).astype(BF16)

    wk_o[0] = rows_out(k_ref[0], C)
    qr_o[0] = rows_out(qr_ref[0], N)
    qi_o[0] = rows_out(qi_ref[0], N)
    rr_o[0] = lanes_out(rr_ref[0])
    ri_o[0] = lanes_out(ri_ref[0])


def _ssm_expand(kcat, qr, qi, rr, ri):
    tiles, lc, _ = kcat.shape
    L, C = SSM_CHUNK, SSM_GROUP
    gt = LANES // C
    w = L * LANES
    ns = qr.shape[2]
    place = np.zeros((L, gt, C, L, C), np.float32)
    for s in range(L):
        for c in range(C):
            place[s, :, c, s, c] = 1.0
    e = jnp.asarray(place.reshape(w, lc), BF16)
    per_tile = lambda a: pl.BlockSpec((1,) + a.shape[1:], lambda j: (j, 0, 0))
    out = lambda r, c: (jax.ShapeDtypeStruct((tiles, r, c), BF16), pl.BlockSpec((1, r, c), lambda j: (j, 0, 0)))
    outs = [out(w, LANES), out(w, ns), out(w, ns), out(ns, w), out(ns, w)]
    return pl.pallas_call(
        _ssm_expand_kernel,
        grid=(tiles,),
        in_specs=[pl.BlockSpec(e.shape, lambda j: (0, 0))] + [per_tile(a) for a in (kcat, qr, qi, rr, ri)],
        out_specs=[o[1] for o in outs],
        out_shape=[o[0] for o in outs],
        compiler_params=_params("parallel"),
        name="ssm_expand",
    )(e, kcat, qr, qi, rr, ri)


def _cmul(ar, ai, xr, xi):
    return ar * xr - ai * xi, ar * xi + ai * xr


def _ssm_kernel(u_ref, wk_ref, qr_ref, qi_ref, rr_ref, ri_ref, al_ref, aseg_ref, d_ref, y_ref,
                uf_ref, uc_ref, xr_ref, xi_ref, *, nch, nseg):
    L = SSM_CHUNK
    seg_rows = nch * L
    rows = nch * nseg
    uf_ref[...] = u_ref[0].astype(F32)
    for c in range(nch):
        for s in range(L):
            uc_ref[c * nseg:(c + 1) * nseg, s * LANES:(s + 1) * LANES] = \
                uf_ref[pl.ds(c * L + s, nseg, stride=seg_rows), :]
    ucb = uc_ref[...].astype(BF16)
    xr_ref[...] = jnp.dot(ucb, qr_ref[0], preferred_element_type=F32)
    xi_ref[...] = jnp.dot(ucb, qi_ref[0], preferred_element_type=F32)
    alr, ali = al_ref[0, 0:1, :], al_ref[0, 1:2, :]

    def scan_body(c, carry):
        sr, si = carry
        r0 = pl.multiple_of(c * nseg, nseg)
        inr = xr_ref[pl.ds(r0, nseg), :]
        ini = xi_ref[pl.ds(r0, nseg), :]
        xr_ref[pl.ds(r0, nseg), :] = sr
        xi_ref[pl.ds(r0, nseg), :] = si
        pr, pi = _cmul(alr, ali, sr, si)
        return pr + inr, pi + ini

    zero = jnp.zeros((nseg, xr_ref.shape[1]), F32)
    er, ei = lax.fori_loop(0, nch, scan_body, (zero, zero))

    row = lax.broadcasted_iota(I32, zero.shape, 0)
    cr, ci = zero, zero
    for _ in range(nseg - 1):
        pr, pi = _cmul(aseg_ref[0, 0:1, :], aseg_ref[0, 1:2, :], cr, ci)
        cr = jnp.where(row >= 1, pltpu.roll(er + pr, 1, 0), 0.0)
        ci = jnp.where(row >= 1, pltpu.roll(ei + pi, 1, 0), 0.0)

    def corr_body(c, carry):
        cr, ci = carry
        r0 = pl.multiple_of(c * nseg, nseg)
        xr_ref[pl.ds(r0, nseg), :] += cr
        xi_ref[pl.ds(r0, nseg), :] += ci
        return _cmul(alr, ali, cr, ci)

    lax.fori_loop(0, nch, corr_body, (cr, ci))

    yc = jnp.dot(xr_ref[...].astype(BF16), rr_ref[0], preferred_element_type=F32)
    yc += jnp.dot(xi_ref[...].astype(BF16), ri_ref[0], preferred_element_type=F32)
    for t in range(L):
        lanes = slice(t * LANES, (t + 1) * LANES)
        y_t = jnp.dot(ucb[:, :(t + 1) * LANES], wk_ref[0, (L - 1 - t) * LANES:, :], preferred_element_type=F32)
        y_t += yc[:, lanes] + uc_ref[:, lanes] * d_ref[0]
        for c in range(nch):
            y_ref[0, pl.ds(c * L + t, nseg, stride=seg_rows), :] = y_t[c * nseg:(c + 1) * nseg]


def _ssm_scan(proj, wk, qr, qi, rr, ri, al, aseg, d_t, *, nch, nseg):
    b, s, n = proj.shape
    tiles = SSM_WIDTH // LANES
    first = (n - SSM_WIDTH) // LANES
    rows = nch * nseg
    w = SSM_CHUNK * LANES
    ns = qr.shape[2]
    per_tile = lambda r, c: pl.BlockSpec((1, r, c), lambda j, bi: (j, 0, 0))
    return pl.pallas_call(
        functools.partial(_ssm_kernel, nch=nch, nseg=nseg),
        grid=(tiles, b),
        in_specs=[pl.BlockSpec((1, s, LANES), lambda j, bi: (bi, 0, first + j)),
                  per_tile(w, LANES), per_tile(w, ns), per_tile(w, ns), per_tile(ns, w), per_tile(ns, w),
                  per_tile(2, ns), per_tile(2, ns), per_tile(1, LANES)],
        out_specs=pl.BlockSpec((1, s, LANES), lambda j, bi: (bi, 0, j)),
        out_shape=jax.ShapeDtypeStruct((b, s, SSM_WIDTH), F32),
        scratch_shapes=[pltpu.VMEM((s, LANES), F32), pltpu.VMEM((rows, w), F32),
                        pltpu.VMEM((rows, ns), F32), pltpu.VMEM((rows, ns), F32)],
        compiler_params=_params("parallel", "parallel"),
        name="ssm_scan",
    )(proj, wk, qr, qi, rr, ri, al, aseg, d_t)


def _ssm_operators(log_dt, lam_re, lam_im, b_re, b_im, c_re, c_im, d, nch):
    L, C, N = SSM_CHUNK, SSM_GROUP, SSM_STATE
    G = log_dt.shape[0]
    dt = jnp.exp(log_dt.astype(F32))[:, None]
    lr = lam_re.astype(F32)
    li = lam_im.astype(F32)

    def apow(tau):
        tau = jnp.asarray(tau, F32)[..., None, None]
        mag = jnp.exp(tau * (lr * dt))
        ang = tau * (li * dt)
        return mag * jnp.cos(ang), mag * jnp.sin(ang)

    pr, pi = apow(np.arange(L + 1))
    den = lr * lr + li * li
    nr = pr[1] - 1.0
    coef_r = ((nr * lr + pi[1] * li) / den)[..., None]
    coef_i = ((pi[1] * lr - nr * li) / den)[..., None]
    br = b_re.astype(F32)
    bi = b_im.astype(F32)
    bbar_r = coef_r * br - coef_i * bi
    bbar_i = coef_r * bi + coef_i * br
    cr = c_re.astype(F32)
    ci = c_im.astype(F32)
    car = cr[None] * pr[:, :, None, :] - ci[None] * pi[:, :, None, :]
    cai = cr[None] * pi[:, :, None, :] + ci[None] * pr[:, :, None, :]
    lcat = jnp.concatenate([car[:L], -cai[:L]], axis=-1)
    lcat = lcat.transpose(1, 0, 2, 3).reshape(G, L * C, 2 * N)
    bcat = jnp.concatenate([bbar_r, bbar_i], axis=1)
    kmat = _ssm_kmat(lcat, bcat).reshape(G, L, C, C)
    gt = LANES // C
    tiles = G // gt
    kcat = kmat.reshape(tiles, gt, L, C, C)[:, :, ::-1].transpose(0, 2, 4, 1, 3).reshape(tiles, L * C, LANES)
    prq = pr[L - 1 - np.arange(L)]
    piq = pi[L - 1 - np.arange(L)]
    qr = prq[..., None] * bbar_r[None] - piq[..., None] * bbar_i[None]
    qi = prq[..., None] * bbar_i[None] + piq[..., None] * bbar_r[None]
    cat_q = lambda q: q.reshape(L, tiles, gt, N, C).transpose(1, 0, 4, 2, 3).reshape(tiles, L * C, gt * N)
    cat_r = lambda r: r.reshape(L, tiles, gt, C, N).transpose(1, 2, 4, 0, 3).reshape(tiles, gt * N, L * C)
    wk, qr, qi, rr, ri = _ssm_expand(*(a.astype(BF16) for a in
                                       (kcat, cat_q(qr), cat_q(qi), cat_r(car[1:]), cat_r(-cai[1:]))))
    sr, si = apow(np.asarray([L * nch]))
    al = jnp.stack([pr[L], pi[L]], axis=1).reshape(tiles, gt, 2, N).transpose(0, 2, 1, 3).reshape(tiles, 2, gt * N)
    aseg = jnp.stack([sr[0], si[0]], axis=1).reshape(tiles, gt, 2, N).transpose(0, 2, 1, 3).reshape(tiles, 2, gt * N)
    d_t = d.astype(F32).reshape(tiles, 1, LANES)
    return wk, qr, qi, rr, ri, al, aseg, d_t


def _mlp_ln(x, w1_ref, w2_ref, g_ref, b_ref):
    xb = x.astype(BF16)
    acc = jnp.zeros(x.shape, F32)
    for f in range(0, w1_ref.shape[1], MLP_FCHUNK):
        h = jnp.dot(xb, w1_ref[:, f:f + MLP_FCHUNK], preferred_element_type=F32)
        h = jnp.maximum(h, 0.0)
        acc += jnp.dot((h * h).astype(BF16), w2_ref[f:f + MLP_FCHUNK, :], preferred_element_type=F32)
    return _layer_norm(ALPHA * x + acc, g_ref[...], b_ref[...])


def _tail_even_kernel(a_ref, y_ref, x_ref, wglu_ref, bglu_ref, wo_ref, g1_ref, b1_ref,
                      w1_ref, w2_ref, g2_ref, b2_ref, o_ref):
    y = jax.nn.gelu(y_ref[...], approximate=True)
    gate = jnp.dot(y.astype(BF16), wglu_ref[...], preferred_element_type=F32) + bglu_ref[...]
    y = y * (1.0 / (1.0 + jnp.exp(-gate)))
    h = jnp.dot(a_ref[...], wo_ref[:SB_WIDTH, :], preferred_element_type=F32)
    h += jnp.dot(y.astype(BF16), wo_ref[SB_WIDTH:, :], preferred_element_type=F32)
    x1 = _layer_norm(ALPHA * x_ref[...] + h, g1_ref[...], b1_ref[...])
    o_ref[...] = _mlp_ln(x1, w1_ref, w2_ref, g2_ref, b2_ref)


def _tail_odd_kernel(o_in_ref, x_ref, wo_ref, g1_ref, b1_ref, w1_ref, w2_ref, g2_ref, b2_ref, o_ref):
    h = jnp.dot(o_in_ref[...], wo_ref[...], preferred_element_type=F32)
    x1 = _layer_norm(ALPHA * x_ref[...] + h, g1_ref[...], b1_ref[...])
    o_ref[...] = _mlp_ln(x1, w1_ref, w2_ref, g2_ref, b2_ref)


def _layer_tail(body, name, rows, consts, tm=512):
    m, d = rows[-1].shape
    row = lambda a: pl.BlockSpec((tm, a.shape[1]), lambda i: (i, 0))
    once = lambda a: pl.BlockSpec(a.shape, lambda i: (0, 0), pipeline_mode=pl.Buffered(1))
    return pl.pallas_call(
        body,
        grid=(m // tm,),
        in_specs=[row(a) for a in rows] + [once(a) for a in consts],
        out_specs=pl.BlockSpec((tm, d), lambda i: (i, 0)),
        out_shape=jax.ShapeDtypeStruct((m, d), F32),
        compiler_params=_params("parallel"),
        name=name,
    )(*rows, *consts)


def _bf16_split3(x):
    x = np.asarray(x, np.float32)
    hi = x.astype(BF16).astype(np.float32)
    mid = (x - hi).astype(BF16).astype(np.float32)
    lo = (x - hi - mid).astype(BF16).astype(np.float32)
    return hi, mid, lo


def _alibi_columns(s):
    slopes = (2.0 ** (-8.0 * np.arange(1, DSA_HEADS + 1) / DSA_HEADS)).astype(np.float32)
    slopes = (slopes.astype(np.float64) * LOG2E).astype(np.float32)
    parts = _bf16_split3(slopes)
    assert np.all(parts[0] + parts[1] + parts[2] == slopes)
    q_cols = np.zeros((DSA_HEADS, 1, HEAD_DIM), np.float32)
    q_cols[:, 0, :6] = np.stack(parts * 2, axis=-1)
    pos = np.arange(s)
    k_cols = np.zeros((s, HEAD_DIM), np.float32)
    k_cols[:, 0:3] = (pos // 64 * 64)[:, None]
    k_cols[:, 3:6] = (pos % 64)[:, None]
    return jnp.asarray(q_cols, BF16), jnp.asarray(k_cols, BF16)


def _key_to_float(key):
    bits = key ^ ((key >> 31) & jnp.int32(0x7FFFFFFF))
    return lax.bitcast_convert_type(bits, F32)


def _dsa_kernel(q_ref, k_ref, v_ref, qi_ref, kiw_ref, kiwq_ref, qcols_ref, kcols_ref, o_ref,
                kext_ref, vt_ref, qie_ref, sc_ref, m_ref, acc_ref, qs_ref, bias_ref, xs_ref,
                *, tq, tkb, tk, topk):
    i = pl.program_id(1)
    t0 = i * tq
    s = k_ref.shape[1]
    dh = HEAD_DIM
    kv_heads = k_ref.shape[2] // dh
    n_heads = q_ref.shape[2] // dh
    rep = n_heads // kv_heads
    n_idx_heads = qi_ref.shape[2] // IDX_DIM

    @pl.when(i == 0)
    def _():
        def prep_chunk(c, carry):
            c0 = pl.multiple_of(c * TR_CHUNK, TR_CHUNK)
            kblk = k_ref[0, pl.ds(c0, TR_CHUNK), :].astype(F32)
            vblk_t = v_ref[0, pl.ds(c0, TR_CHUNK), :].astype(F32).T
            for g in range(kv_heads):
                kext_ref[g, pl.ds(c0, TR_CHUNK), 0:dh] = kblk[:, g * dh:(g + 1) * dh].astype(BF16)
                kext_ref[g, pl.ds(c0, TR_CHUNK), dh:2 * dh] = kcols_ref[pl.ds(c0, TR_CHUNK), :]
                vt_ref[g, 0:dh, pl.ds(c0, TR_CHUNK)] = vblk_t[g * dh:(g + 1) * dh].astype(BF16)
            return carry
        lax.fori_loop(0, s // TR_CHUNK, prep_chunk, 0)
        ones_row = lax.broadcasted_iota(I32, (BF16_SUBLANES, s), 0) == 0
        for g in range(kv_heads):
            vt_ref[g, dh:dh + BF16_SUBLANES, :] = jnp.where(ones_row, 1.0, 0.0).astype(BF16)

    qf = q_ref[0].astype(F32) * (QK_SCALE * LOG2E)
    for h in range(n_heads):
        qs_ref[h, :, 0:dh] = qf[:, h * dh:(h + 1) * dh].astype(BF16)
        qs_ref[h, :, dh:2 * dh] = jnp.broadcast_to(qcols_ref[h], (tq, dh))
    qif = qi_ref[0].astype(F32)
    lane = lax.broadcasted_iota(I32, (tq, LANES), 1)
    per_tile = LANES // IDX_DIM
    for h in range(n_idx_heads):
        tile = qif[:, (h // per_tile) * LANES:(h // per_tile + 1) * LANES]
        shift = (h % per_tile) * IDX_DIM
        if shift:
            tile = pltpu.roll(tile, LANES - shift, 1)
        qie_ref[h] = jnp.where(lane < IDX_DIM, tile, 0.0).astype(BF16)
    wt = kiwq_ref[0].astype(F32).T[IDX_DIM:IDX_DIM + n_idx_heads] * IDX_SCALE

    nsb = (t0 + tq + tkb - 1) // tkb
    srow = lax.broadcasted_iota(I32, (tkb, tq), 0)
    tcol = t0 + lax.broadcasted_iota(I32, (tkb, tq), 1)

    def score_body(jb, carry):
        s0 = pl.multiple_of(jb * tkb, tkb)
        kib = kiw_ref[0, pl.ds(s0, tkb), :]
        sc = jnp.zeros((tkb, tq), F32)
        for h in range(n_idx_heads):
            z = lax.dot_general(kib, qie_ref[h], _NT, preferred_element_type=F32)
            sc += wt[h:h + 1, :] * jnp.maximum(z, 0.0)
        sc_ref[pl.ds(s0, tkb), :] = jnp.where(s0 + srow <= tcol, sc, -jnp.inf)
        return carry

    lax.fori_loop(0, nsb, score_body, 0)

    def count(pred):
        def body(jb, acc):
            s0 = pl.multiple_of(jb * tkb, tkb)
            ind = jnp.where(pred(sc_ref[pl.ds(s0, tkb), :], s0 + srow), 1.0, 0.0)
            return acc + jnp.sum(ind.reshape(tkb // COUNT_ROWS, COUNT_ROWS, tq), axis=0)
        acc = lax.fori_loop(0, nsb, body, jnp.zeros((COUNT_ROWS, tq), F32))
        return jnp.sum(acc, axis=0, keepdims=True)

    kf = float(topk)
    c0 = count(lambda sc, sidx: sc >= 0.0)
    ok0 = c0 >= kf
    thr_key0 = jnp.where(ok0, jnp.int32(0), jnp.int32(INT_MIN))
    cnt0 = jnp.where(ok0, c0, 0.0)

    def thr_cond(carry):
        it, _, cnt = carry
        return jnp.logical_and(it < 31, jnp.max(jnp.abs(cnt - kf)) > 0.0)

    def bit_body(it, carry):
        thr_key, cnt = carry
        cand = thr_key | (jnp.int32(1) << (30 - it))
        cand_f = _key_to_float(cand)
        c = count(lambda sc, sidx: sc >= cand_f)
        ok = c >= kf
        return jnp.where(ok, cand, thr_key), jnp.where(ok, c, cnt)

    def thr_body(carry):
        it, thr_key, cnt = carry
        nxt = jnp.minimum(it + BITS_PER_CHECK, 31)
        thr_key, cnt = lax.fori_loop(it, nxt, bit_body, (thr_key, cnt))
        return nxt, thr_key, cnt

    _, thr_key, n_ge = lax.while_loop(thr_cond, thr_body, (jnp.int32(0), thr_key0, cnt0))
    thr = jnp.where(thr_key == INT_MIN, -jnp.inf, _key_to_float(thr_key))
    has_ties = jnp.max(n_ge) > kf
    tie_passes = jnp.where(has_ties, 1, 0)
    n_gt = lax.fori_loop(0, tie_passes, lambda it, c: count(lambda sc, sidx: sc > thr), jnp.zeros((1, tq), F32))
    need = kf - n_gt

    def cut_body(it, cut):
        cand = cut | (jnp.int32(1) << (12 - it))
        c = count(lambda sc, sidx: jnp.logical_and(sc == thr, sidx < cand))
        return jnp.where(c <= need, cand, cut)

    cut = lax.fori_loop(0, jnp.where(has_ties, 13, 0), cut_body, jnp.zeros((1, tq), I32))
    cut = jnp.where(has_ties, cut, jnp.int32(2 ** 30))

    nkb = (t0 + tq + tk - 1) // tk
    srow_k = lax.broadcasted_iota(I32, (tk, tq), 0)
    tcol_k = t0 + lax.broadcasted_iota(I32, (tk, tq), 1)

    def set_block_bias(jb):
        s0 = pl.multiple_of(jb * tk, tk)
        sc = sc_ref[pl.ds(s0, tk), :]
        sidx = s0 + srow_k
        sel = jnp.logical_or(sc > thr, jnp.logical_and(sc == thr, sidx < cut))
        sel = jnp.logical_and(sel, sidx <= tcol_k)
        bias_ref[...] = jnp.where(sel, 0.0, NEG_BIG)
        return s0

    def logits(h, kb):
        return lax.dot_general(kb, qs_ref[h], _NT, preferred_element_type=F32)

    m_ref[...] = jnp.full(m_ref.shape, NEG_BIG, F32)
    acc_ref[...] = jnp.zeros_like(acc_ref)

    def attn_body(jb, carry):
        s0 = set_block_bias(jb)
        maxima = []
        for g in range(kv_heads):
            kb = kext_ref[g, pl.ds(s0, tk), :]
            for r in range(rep):
                h = g * rep + r
                x = logits(h, kb) + bias_ref[...]
                xs_ref[h] = x
                maxima.append(jnp.max(x, axis=0, keepdims=True))
        for g in range(kv_heads):
            vt = vt_ref[g, :, pl.ds(s0, tk)]
            for r in range(rep):
                h = g * rep + r
                m_prev = m_ref[h]
                m_new = jnp.maximum(m_prev, maxima[h])
                p = jnp.exp2(xs_ref[h] - m_new).astype(BF16)
                acc_ref[h] = jnp.exp2(m_prev - m_new) * acc_ref[h] + jnp.dot(
                    vt, p, preferred_element_type=F32)
                m_ref[h] = m_new
        return carry

    lax.fori_loop(0, nkb, attn_body, 0)
    for p in range(n_heads // 2):
        outs = []
        for h in (2 * p, 2 * p + 1):
            acc = acc_ref[h]
            outs.append(acc[:dh] / acc[dh:dh + 1])
        o_ref[0, :, p * LANES:(p + 1) * LANES] = jnp.concatenate(outs, axis=0).T.astype(o_ref.dtype)


def _dsa_attention(proj, topk, tq=256, tkb=512, tk=256):
    b, s, n = proj.shape
    dh = HEAD_DIM
    qw = DSA_HEADS * dh
    kvw = DSA_KV_HEADS * dh
    iw = IDX_HEADS * IDX_DIM
    assert qw % kvw == 0 and (qw + 2 * kvw) % iw == 0 and (qw + 2 * kvw + iw) % LANES == 0
    assert IDX_DIM + IDX_HEADS <= LANES and n == qw + 2 * kvw + iw + LANES
    q_cols, k_cols = _alibi_columns(s)
    tkb = min(tkb, s)
    tk = min(tk, s)
    kiw_block = (qw + 2 * kvw + iw) // LANES
    dv = dh + BF16_SUBLANES
    return pl.pallas_call(
        functools.partial(_dsa_kernel, tq=tq, tkb=tkb, tk=tk, topk=topk),
        grid=(b, s // tq),
        in_specs=[pl.BlockSpec((1, tq, qw), lambda bi, i: (bi, i, 0)),
                  pl.BlockSpec((1, s, kvw), lambda bi, i: (bi, 0, qw // kvw)),
                  pl.BlockSpec((1, s, kvw), lambda bi, i: (bi, 0, qw // kvw + 1)),
                  pl.BlockSpec((1, tq, iw), lambda bi, i: (bi, i, (qw + 2 * kvw) // iw)),
                  pl.BlockSpec((1, s, LANES), lambda bi, i: (bi, 0, kiw_block)),
                  pl.BlockSpec((1, tq, LANES), lambda bi, i: (bi, i, kiw_block)),
                  pl.BlockSpec((DSA_HEADS, 1, dh), lambda bi, i: (0, 0, 0)),
                  pl.BlockSpec((s, dh), lambda bi, i: (0, 0))],
        out_specs=pl.BlockSpec((1, tq, qw), lambda bi, i: (bi, i, 0)),
        out_shape=jax.ShapeDtypeStruct((b, s, qw), BF16),
        scratch_shapes=[pltpu.VMEM((DSA_KV_HEADS, s, 2 * dh), BF16),
                        pltpu.VMEM((DSA_KV_HEADS, dv, s), BF16),
                        pltpu.VMEM((IDX_HEADS, tq, LANES), BF16),
                        pltpu.VMEM((s, tq), F32),
                        pltpu.VMEM((DSA_HEADS, 1, tq), F32),
                        pltpu.VMEM((DSA_HEADS, dv, tq), F32),
                        pltpu.VMEM((DSA_HEADS, tq, 2 * dh), BF16),
                        pltpu.VMEM((tk, tq), F32),
                        pltpu.VMEM((DSA_HEADS, tk, tq), F32)],
        compiler_params=_params("parallel", "arbitrary"),
        name="dsa_attention",
    )(proj, proj, proj, proj, proj, proj, q_cols, k_cols)


def _row(v):
    return v.astype(F32)[None]


def _even_layer(x2d, bsz, s, w_in, ssm_ops, w_glu, b_glu, w_out, g_mix, b_mix, mlp):
    proj = _proj(x2d, w_in.astype(BF16)).reshape(bsz, s, -1)
    a_out = _sb_attention(proj).reshape(bsz * s, SB_WIDTH)
    y = _ssm_scan(proj, *ssm_ops, nch=s // (SSM_CHUNK * SUBLANES), nseg=SUBLANES).reshape(bsz * s, SSM_WIDTH)
    consts = (w_glu.astype(BF16), _row(b_glu), w_out.astype(BF16), _row(g_mix), _row(b_mix)) + mlp
    return _layer_tail(_tail_even_kernel, "tail_even", (a_out, y, x2d), consts)


def _odd_layer(x2d, bsz, s, w_in, w_out, g_mix, b_mix, mlp):
    proj = _proj(x2d, w_in.astype(BF16)).reshape(bsz, s, -1)
    o = _dsa_attention(proj, min(TOPK_MAX, s // 4)).reshape(bsz * s, DSA_HEADS * HEAD_DIM)
    consts = (w_out.astype(BF16), _row(g_mix), _row(b_mix)) + mlp
    return _layer_tail(_tail_odd_kernel, "tail_odd", (o, x2d), consts)


def kernel(x, sb_ssm_w_in, ssm_log_dt, ssm_lam_re, ssm_lam_im, ssm_b_re, ssm_b_im, ssm_c_re, ssm_c_im, ssm_d,
           ssm_w_glu, ssm_b_glu, sb_ssm_w_out, dsa_w_in, dsa_w_out, ln_mix_g, ln_mix_b, ln_ffn_g, ln_ffn_b,
           mlp_w1, mlp_w2):
    bsz, s, d = x.shape
    x2d = x.reshape(bsz * s, d).astype(F32)
    depth = mlp_w1.shape[0]
    stacked = [p.reshape((-1,) + p.shape[2:]) for p in
               (ssm_log_dt, ssm_lam_re, ssm_lam_im, ssm_b_re, ssm_b_im, ssm_c_re, ssm_c_im, ssm_d)]
    ssm_ops = _ssm_operators(*stacked, nch=s // (SSM_CHUNK * SUBLANES))
    tiles = SSM_WIDTH // LANES
    for i in range(depth):
        j = i // 2
        mlp = (mlp_w1[i].astype(BF16), mlp_w2[i].astype(BF16), _row(ln_ffn_g[i]), _row(ln_ffn_b[i]))
        if i % 2 == 0:
            layer_ops = tuple(op[j * tiles:(j + 1) * tiles] for op in ssm_ops)
            x2d = _even_layer(x2d, bsz, s, sb_ssm_w_in[j], layer_ops, ssm_w_glu[j], ssm_b_glu[j],
                              sb_ssm_w_out[j], ln_mix_g[i], ln_mix_b[i], mlp)
        else:
            n_in = dsa_w_in.shape[2]
            pad = (-n_in) % LANES
            w_in = jnp.pad(dsa_w_in[j], ((0, 0), (0, pad)))
            x2d = _odd_layer(x2d, bsz, s, w_in, dsa_w_out[j], ln_mix_g[i], ln_mix_b[i], mlp)
    return x2d.reshape(bsz, s, d).astype(x.dtype)
```

```python
import functools
import math

import jax
import jax.numpy as jnp
import numpy as np
from jax import lax
from jax.experimental import pallas as pl
from jax.experimental.pallas import tpu as pltpu

F32 = jnp.float32
BF16 = jnp.bfloat16
I32 = jnp.int32

HEAD_DIM = 64
SB_WIDTH = 512
SB_HEADS = SB_WIDTH // HEAD_DIM
SSM_WIDTH = 512
SSM_GROUP = 16
SSM_STATE = 64
DSA_HEADS = 16
DSA_KV_HEADS = 4
IDX_HEADS = 8
IDX_DIM = 32
TOPK_MAX = 256
DEPTH = 4
ALPHA = (2 * DEPTH) ** 0.25
LN_EPS = 1e-5
IDX_SCALE = (IDX_DIM ** -0.5) * (IDX_HEADS ** -0.5)
QK_SCALE = HEAD_DIM ** -0.5

SSM_CHUNK = 16
SUBLANES = 8
BF16_SUBLANES = 16
LANES = 128
VMEM_LIMIT = 56 * 1024 * 1024
EXP2_UNDERFLOW = -150.0
NEG_BIG = -1e30
INT_MIN = -2 ** 31
COUNT_ROWS = 64
TR_CHUNK = 256
LOG2E = math.log2(math.e)
MLP_FCHUNK = 1024
BITS_PER_CHECK = 8

_NT = (((1,), (1,)), ((), ()))


def _params(*sem):
    return pltpu.CompilerParams(dimension_semantics=sem, vmem_limit_bytes=VMEM_LIMIT)


def _layer_norm(r, g, b):
    mu = jnp.mean(r, axis=-1, keepdims=True)
    c = r - mu
    var = jnp.mean(c * c, axis=-1, keepdims=True)
    return c * lax.rsqrt(var + LN_EPS) * g + b


def _proj_kernel(x_ref, w_ref, o_ref):
    o_ref[...] = jnp.dot(x_ref[...].astype(BF16), w_ref[...],
                         preferred_element_type=F32).astype(o_ref.dtype)


def _proj(x2d, w, tm=512):
    m, k = x2d.shape
    n = w.shape[1]
    return pl.pallas_call(
        _proj_kernel,
        grid=(m // tm,),
        in_specs=[pl.BlockSpec((tm, k), lambda i: (i, 0)),
                  pl.BlockSpec((k, n), lambda i: (0, 0))],
        out_specs=pl.BlockSpec((tm, n), lambda i: (i, 0)),
        out_shape=jax.ShapeDtypeStruct((m, n), BF16),
        compiler_params=_params("parallel"),
        name="proj",
    )(x2d, w)


def _sb_kernel(q_ref, k_ref, v_ref, o_ref, vt_ref, qm_ref, acc_ref, a_ref, *, tq, tk):
    i = pl.program_id(1)
    s = k_ref.shape[1]
    dh = HEAD_DIM
    heads = q_ref.shape[2] // dh
    pairs = heads // 2

    @pl.when(i == 0)
    def _():
        def transpose_chunk(c, carry):
            c0 = pl.multiple_of(c * TR_CHUNK, TR_CHUNK)
            vt_ref[:, pl.ds(c0, TR_CHUNK)] = v_ref[0, pl.ds(c0, TR_CHUNK), :].astype(F32).T.astype(BF16)
            return carry
        lax.fori_loop(0, s // TR_CHUNK, transpose_chunk, 0)

    lane = lax.broadcasted_iota(I32, (tq, LANES), 1)
    for p in range(pairs):
        qp = q_ref[0, :, p * LANES:(p + 1) * LANES].astype(F32) * (QK_SCALE * LOG2E)
        qm_ref[2 * p] = jnp.where(lane < dh, qp, 0.0).astype(BF16)
        qm_ref[2 * p + 1] = jnp.where(lane >= dh, qp, 0.0).astype(BF16)
    acc_ref[...] = jnp.zeros_like(acc_ref)
    a_ref[...] = jnp.zeros_like(a_ref)
    rows = lax.broadcasted_iota(I32, (tk, tq), 0)
    tcol = i * tq + lax.broadcasted_iota(I32, (tk, tq), 1)
    tri = lax.broadcasted_iota(I32, (tk, tk), 1) > lax.broadcasted_iota(I32, (tk, tk), 0)
    tri = jnp.where(tri, 1.0, 0.0).astype(BF16)
    tri2 = jnp.concatenate([tri, tri], axis=1)

    def body(carry):
        j, _ = carry
        k0 = pl.multiple_of(j * tk, tk)
        mask = (k0 + rows) < tcol
        hs = range(heads)
        kbs = [k_ref[0, pl.ds(k0, tk), p * LANES:(p + 1) * LANES] for p in range(pairs)]
        zs = [lax.dot_general(kbs[h // 2], qm_ref[h], _NT, preferred_element_type=F32) for h in hs]
        sps = [jnp.log2(1.0 + jnp.exp2(-jnp.abs(z))) for z in zs]
        log_betas = [jnp.minimum(z, 0.0) - sp for z, sp in zip(zs, sps)]
        l1s = [jnp.where(mask, -jnp.maximum(z, 0.0) - sp, 0.0) for z, sp in zip(zs, sps)]
        his = [l1.astype(BF16) for l1 in l1s]
        los = [(l1 - hi.astype(F32)).astype(BF16) for l1, hi in zip(l1s, his)]
        es = [jnp.dot(tri2, jnp.concatenate([hi, lo], axis=0), preferred_element_type=F32)
              for hi, lo in zip(his, los)]
        a_olds = [a_ref[h] for h in hs]
        ws = [jnp.where(mask, jnp.exp2(lb + a + e), 0.0).astype(BF16) for lb, a, e in zip(log_betas, a_olds, es)]
        amax = None
        for h in hs:
            acc_ref[h] += jnp.dot(vt_ref[h * dh:(h + 1) * dh, pl.ds(k0, tk)], ws[h], preferred_element_type=F32)
            a_new = a_olds[h] + jnp.sum(l1s[h], axis=0, keepdims=True)
            a_ref[h] = a_new
            amax = a_new if amax is None else jnp.maximum(amax, a_new)
        return j - 1, jnp.max(amax) > EXP2_UNDERFLOW

    j0 = ((i + 1) * tq - 1) // tk
    lax.while_loop(lambda c: jnp.logical_and(c[0] >= 0, c[1]), body, (j0, True))
    for p in range(pairs):
        both = jnp.concatenate([acc_ref[2 * p], acc_ref[2 * p + 1]], axis=0)
        o_ref[0, :, p * LANES:(p + 1) * LANES] = both.T.astype(o_ref.dtype)


def _sb_attention(proj, tq=256, tk=128):
    b, s, _ = proj.shape
    w = SB_WIDTH
    return pl.pallas_call(
        functools.partial(_sb_kernel, tq=tq, tk=tk),
        grid=(b, s // tq),
        in_specs=[pl.BlockSpec((1, tq, w), lambda bi, i: (bi, i, 0)),
                  pl.BlockSpec((1, s, w), lambda bi, i: (bi, 0, 1)),
                  pl.BlockSpec((1, s, w), lambda bi, i: (bi, 0, 2))],
        out_specs=pl.BlockSpec((1, tq, w), lambda bi, i: (bi, i, 0)),
        out_shape=jax.ShapeDtypeStruct((b, s, w), BF16),
        scratch_shapes=[pltpu.VMEM((w, s), BF16),
                        pltpu.VMEM((SB_HEADS, tq, LANES), BF16),
                        pltpu.VMEM((SB_HEADS, HEAD_DIM, tq), F32),
                        pltpu.VMEM((SB_HEADS, 1, tq), F32)],
        compiler_params=_params("parallel", "arbitrary"),
        name="sb_attention",
    )(proj, proj, proj)


def _ssm_kmat_kernel(l_ref, b_ref, o_ref):
    for g in range(l_ref.shape[0]):
        o_ref[g] = jnp.dot(l_ref[g], b_ref[g], precision=lax.Precision.HIGHEST,
                           preferred_element_type=F32)


def _ssm_kmat(lcat, bcat, gb=8):
    g, r, n2 = lcat.shape
    c = bcat.shape[2]
    return pl.pallas_call(
        _ssm_kmat_kernel,
        grid=(g // gb,),
        in_specs=[pl.BlockSpec((gb, r, n2), lambda i: (i, 0, 0)),
                  pl.BlockSpec((gb, n2, c), lambda i: (i, 0, 0))],
        out_specs=pl.BlockSpec((gb, r, c), lambda i: (i, 0, 0)),
        out_shape=jax.ShapeDtypeStruct((g, r, c), F32),
        compiler_params=_params("parallel"),
        name="ssm_kmat",
    )(lcat, bcat)


def _ssm_expand_kernel(e_ref, k_ref, qr_ref, qi_ref, rr_ref, ri_ref, wk_o, qr_o, qi_o, rr_o, ri_o):
    C, N, gt = SSM_GROUP, SSM_STATE, LANES // SSM_GROUP
    e = e_ref[...]

    def rows_out(x, lanes_per_group):
        y = jnp.dot(e, x, preferred_element_type=F32)
        row_group = (lax.broadcasted_iota(I32, y.shape, 0) // C) % gt
        lane_group = lax.broadcasted_iota(I32, y.shape, 1) // lanes_per_group
        return jnp.where(row_group == lane_group, y, 0.0).astype(BF16)

    def lanes_out(x):
        y = lax.dot_general(x, e, _NT, preferred_element_type=F32)
        row_group = lax.broadcasted_iota(I32, y.shape, 0) // N
        lane_group = (lax.broadcasted_iota(I32, y.shape, 1) // C) % gt
        return jnp.where(row_group == lane_group, y, 0.0).astype(BF16)

    wk_o[0] = rows_out(k_ref[0], C)
    qr_o[0] = rows_out(qr_ref[0], N)
    qi_o[0] = rows_out(qi_ref[0], N)
    rr_o[0] = lanes_out(rr_ref[0])
    ri_o[0] = lanes_out(ri_ref[0])


def _ssm_expand(kcat, qr, qi, rr, ri):
    tiles, lc, _ = kcat.shape
    L, C = SSM_CHUNK, SSM_GROUP
    gt = LANES // C
    w = L * LANES
    ns = qr.shape[2]
    place = np.zeros((L, gt, C, L, C), np.float32)
    for s in range(L):
        for c in range(C):
            place[s, :, c, s, c] = 1.0
    e = jnp.asarray(place.reshape(w, lc), BF16)
    per_tile = lambda a: pl.BlockSpec((1,) + a.shape[1:], lambda j: (j, 0, 0))
    out = lambda r, c: (jax.ShapeDtypeStruct((tiles, r, c), BF16), pl.BlockSpec((1, r, c), lambda j: (j, 0, 0)))
    outs = [out(w, LANES), out(w, ns), out(w, ns), out(ns, w), out(ns, w)]
    return pl.pallas_call(
        _ssm_expand_kernel,
        grid=(tiles,),
        in_specs=[pl.BlockSpec(e.shape, lambda j: (0, 0))] + [per_tile(a) for a in (kcat, qr, qi, rr, ri)],
        out_specs=[o[1] for o in outs],
        out_shape=[o[0] for o in outs],
        compiler_params=_params("parallel"),
        name="ssm_expand",
    )(e, kcat, qr, qi, rr, ri)


def _cmul(ar, ai, xr, xi):
    return ar * xr - ai * xi, ar * xi + ai * xr


def _ssm_kernel(u_ref, wk_ref, qr_ref, qi_ref, rr_ref, ri_ref, al_ref, aseg_ref, d_ref, y_ref,
                uf_ref, uc_ref, xr_ref, xi_ref, *, nch, nseg):
    L = SSM_CHUNK
    seg_rows = nch * L
    rows = nch * nseg
    uf_ref[...] = u_ref[0].astype(F32)
    for c in range(nch):
        for s in range(L):
            uc_ref[c * nseg:(c + 1) * nseg, s * LANES:(s + 1) * LANES] = \
                uf_ref[pl.ds(c * L + s, nseg, stride=seg_rows), :]
    ucb = uc_ref[...].astype(BF16)
    xr_ref[...] = jnp.dot(ucb, qr_ref[0], preferred_element_type=F32)
    xi_ref[...] = jnp.dot(ucb, qi_ref[0], preferred_element_type=F32)
    alr, ali = al_ref[0, 0:1, :], al_ref[0, 1:2, :]

    def scan_body(c, carry):
        sr, si = carry
        r0 = pl.multiple_of(c * nseg, nseg)
        inr = xr_ref[pl.ds(r0, nseg), :]
        ini = xi_ref[pl.ds(r0, nseg), :]
        xr_ref[pl.ds(r0, nseg), :] = sr
        xi_ref[pl.ds(r0, nseg), :] = si
        pr, pi = _cmul(alr, ali, sr, si)
        return pr + inr, pi + ini

    zero = jnp.zeros((nseg, xr_ref.shape[1]), F32)
    er, ei = lax.fori_loop(0, nch, scan_body, (zero, zero))

    row = lax.broadcasted_iota(I32, zero.shape, 0)
    cr, ci = zero, zero
    for _ in range(nseg - 1):
        pr, pi = _cmul(aseg_ref[0, 0:1, :], aseg_ref[0, 1:2, :], cr, ci)
        cr = jnp.where(row >= 1, pltpu.roll(er + pr, 1, 0), 0.0)
        ci = jnp.where(row >= 1, pltpu.roll(ei + pi, 1, 0), 0.0)

    def corr_body(c, carry):
        cr, ci = carry
        r0 = pl.multiple_of(c * nseg, nseg)
        xr_ref[pl.ds(r0, nseg), :] += cr
        xi_ref[pl.ds(r0, nseg), :] += ci
        return _cmul(alr, ali, cr, ci)

    lax.fori_loop(0, nch, corr_body, (cr, ci))

    yc = jnp.dot(xr_ref[...].astype(BF16), rr_ref[0], preferred_element_type=F32)
    yc += jnp.dot(xi_ref[...].astype(BF16), ri_ref[0], preferred_element_type=F32)
    for t in range(L):
        lanes = slice(t * LANES, (t + 1) * LANES)
        y_t = jnp.dot(ucb[:, :(t + 1) * LANES], wk_ref[0, (L - 1 - t) * LANES:, :], preferred_element_type=F32)
        y_t += yc[:, lanes] + uc_ref[:, lanes] * d_ref[0]
        for c in range(nch):
            y_ref[0, pl.ds(c * L + t, nseg, stride=seg_rows), :] = y_t[c * nseg:(c + 1) * nseg]


def _ssm_scan(proj, wk, qr, qi, rr, ri, al, aseg, d_t, *, nch, nseg):
    b, s, n = proj.shape
    tiles = SSM_WIDTH // LANES
    first = (n - SSM_WIDTH) // LANES
    rows = nch * nseg
    w = SSM_CHUNK * LANES
    ns = qr.shape[2]
    per_tile = lambda r, c: pl.BlockSpec((1, r, c), lambda j, bi: (j, 0, 0))
    return pl.pallas_call(
        functools.partial(_ssm_kernel, nch=nch, nseg=nseg),
        grid=(tiles, b),
        in_specs=[pl.BlockSpec((1, s, LANES), lambda j, bi: (bi, 0, first + j)),
                  per_tile(w, LANES), per_tile(w, ns), per_tile(w, ns), per_tile(ns, w), per_tile(ns, w),
                  per_tile(2, ns), per_tile(2, ns), per_tile(1, LANES)],
        out_specs=pl.BlockSpec((1, s, LANES), lambda j, bi: (bi, 0, j)),
        out_shape=jax.ShapeDtypeStruct((b, s, SSM_WIDTH), F32),
        scratch_shapes=[pltpu.VMEM((s, LANES), F32), pltpu.VMEM((rows, w), F32),
                        pltpu.VMEM((rows, ns), F32), pltpu.VMEM((rows, ns), F32)],
        compiler_params=_params("parallel", "parallel"),
        name="ssm_scan",
    )(proj, wk, qr, qi, rr, ri, al, aseg, d_t)


def _ssm_operators(log_dt, lam_re, lam_im, b_re, b_im, c_re, c_im, d, nch):
    L, C, N = SSM_CHUNK, SSM_GROUP, SSM_STATE
    G = log_dt.shape[0]
    dt = jnp.exp(log_dt.astype(F32))[:, None]
    lr = lam_re.astype(F32)
    li = lam_im.astype(F32)

    def apow(tau):
        tau = jnp.asarray(tau, F32)[..., None, None]
        mag = jnp.exp(tau * (lr * dt))
        ang = tau * (li * dt)
        return mag * jnp.cos(ang), mag * jnp.sin(ang)

    pr, pi = apow(np.arange(L + 1))
    den = lr * lr + li * li
    nr = pr[1] - 1.0
    coef_r = ((nr * lr + pi[1] * li) / den)[..., None]
    coef_i = ((pi[1] * lr - nr * li) / den)[..., None]
    br = b_re.astype(F32)
    bi = b_im.astype(F32)
    bbar_r = coef_r * br - coef_i * bi
    bbar_i = coef_r * bi + coef_i * br
    cr = c_re.astype(F32)
    ci = c_im.astype(F32)
    car = cr[None] * pr[:, :, None, :] - ci[None] * pi[:, :, None, :]
    cai = cr[None] * pi[:, :, None, :] + ci[None] * pr[:, :, None, :]
    lcat = jnp.concatenate([car[:L], -cai[:L]], axis=-1)
    lcat = lcat.transpose(1, 0, 2, 3).reshape(G, L * C, 2 * N)
    bcat = jnp.concatenate([bbar_r, bbar_i], axis=1)
    kmat = _ssm_kmat(lcat, bcat).reshape(G, L, C, C)
    gt = LANES // C
    tiles = G // gt
    kcat = kmat.reshape(tiles, gt, L, C, C)[:, :, ::-1].transpose(0, 2, 4, 1, 3).reshape(tiles, L * C, LANES)
    prq = pr[L - 1 - np.arange(L)]
    piq = pi[L - 1 - np.arange(L)]
    qr = prq[..., None] * bbar_r[None] - piq[..., None] * bbar_i[None]
    qi = prq[..., None] * bbar_i[None] + piq[..., None] * bbar_r[None]
    cat_q = lambda q: q.reshape(L, tiles, gt, N, C).transpose(1, 0, 4, 2, 3).reshape(tiles, L * C, gt * N)
    cat_r = lambda r: r.reshape(L, tiles, gt, C, N).transpose(1, 2, 4, 0, 3).reshape(tiles, gt * N, L * C)
    wk, qr, qi, rr, ri = _ssm_expand(*(a.astype(BF16) for a in
                                       (kcat, cat_q(qr), cat_q(qi), cat_r(car[1:]), cat_r(-cai[1:]))))
    sr, si = apow(np.asarray([L * nch]))
    al = jnp.stack([pr[L], pi[L]], axis=1).reshape(tiles, gt, 2, N).transpose(0, 2, 1, 3).reshape(tiles, 2, gt * N)
    aseg = jnp.stack([sr[0], si[0]], axis=1).reshape(tiles, gt, 2, N).transpose(0, 2, 1, 3).reshape(tiles, 2, gt * N)
    d_t = d.astype(F32).reshape(tiles, 1, LANES)
    return wk, qr, qi, rr, ri, al, aseg, d_t


def _mlp_ln(x, w1_ref, w2_ref, g_ref, b_ref):
    xb = x.astype(BF16)
    acc = jnp.zeros(x.shape, F32)
    for f in range(0, w1_ref.shape[1], MLP_FCHUNK):
        h = jnp.dot(xb, w1_ref[:, f:f + MLP_FCHUNK], preferred_element_type=F32)
        h = jnp.maximum(h, 0.0)
        acc += jnp.dot((h * h).astype(BF16), w2_ref[f:f + MLP_FCHUNK, :], preferred_element_type=F32)
    return _layer_norm(ALPHA * x + acc, g_ref[...], b_ref[...])


def _tail_even_kernel(a_ref, y_ref, x_ref, wglu_ref, bglu_ref, wo_ref, g1_ref, b1_ref,
                      w1_ref, w2_ref, g2_ref, b2_ref, o_ref):
    y = jax.nn.gelu(y_ref[...], approximate=True)
    gate = jnp.dot(y.astype(BF16), wglu_ref[...], preferred_element_type=F32) + bglu_ref[...]
    y = y * (1.0 / (1.0 + jnp.exp(-gate)))
    h = jnp.dot(a_ref[...], wo_ref[:SB_WIDTH, :], preferred_element_type=F32)
    h += jnp.dot(y.astype(BF16), wo_ref[SB_WIDTH:, :], preferred_element_type=F32)
    x1 = _layer_norm(ALPHA * x_ref[...] + h, g1_ref[...], b1_ref[...])
    o_ref[...] = _mlp_ln(x1, w1_ref, w2_ref, g2_ref, b2_ref)


def _tail_odd_kernel(o_in_ref, x_ref, wo_ref, g1_ref, b1_ref, w1_ref, w2_ref, g2_ref, b2_ref, o_ref):
    h = jnp.dot(o_in_ref[...], wo_ref[...], preferred_element_type=F32)
    x1 = _layer_norm(ALPHA * x_ref[...] + h, g1_ref[...], b1_ref[...])
    o_ref[...] = _mlp_ln(x1, w1_ref, w2_ref, g2_ref, b2_ref)


def _layer_tail(body, name, rows, consts, tm=512):
    m, d = rows[-1].shape
    row = lambda a: pl.BlockSpec((tm, a.shape[1]), lambda i: (i, 0))
    once = lambda a: pl.BlockSpec(a.shape, lambda i: (0, 0), pipeline_mode=pl.Buffered(1))
    return pl.pallas_call(
        body,
        grid=(m // tm,),
        in_specs=[row(a) for a in rows] + [once(a) for a in consts],
        out_specs=pl.BlockSpec((tm, d), lambda i: (i, 0)),
        out_shape=jax.ShapeDtypeStruct((m, d), F32),
        compiler_params=_params("parallel"),
        name=name,
    )(*rows, *consts)


def _bf16_split3(x):
    x = np.asarray(x, np.float32)
    hi = x.astype(BF16).astype(np.float32)
    mid = (x - hi).astype(BF16).astype(np.float32)
    lo = (x - hi - mid).astype(BF16).astype(np.float32)
    return hi, mid, lo


def _alibi_columns(s):
    slopes = (2.0 ** (-8.0 * np.arange(1, DSA_HEADS + 1) / DSA_HEADS)).astype(np.float32)
    slopes = (slopes.astype(np.float64) * LOG2E).astype(np.float32)
    parts = _bf16_split3(slopes)
    assert np.all(parts[0] + parts[1] + parts[2] == slopes)
    q_cols = np.zeros((DSA_HEADS, 1, HEAD_DIM), np.float32)
    q_cols[:, 0, :6] = np.stack(parts * 2, axis=-1)
    pos = np.arange(s)
    k_cols = np.zeros((s, HEAD_DIM), np.float32)
    k_cols[:, 0:3] = (pos // 64 * 64)[:, None]
    k_cols[:, 3:6] = (pos % 64)[:, None]
    return jnp.asarray(q_cols, BF16), jnp.asarray(k_cols, BF16)


def _key_to_float(key):
    bits = key ^ ((key >> 31) & jnp.int32(0x7FFFFFFF))
    return lax.bitcast_convert_type(bits, F32)


def _dsa_kernel(q_ref, k_ref, v_ref, qi_ref, kiw_ref, kiwq_ref, qcols_ref, kcols_ref, o_ref,
                kext_ref, vt_ref, qie_ref, sc_ref, m_ref, acc_ref, qs_ref, bias_ref, xs_ref,
                *, tq, tkb, tk, topk):
    i = pl.program_id(1)
    t0 = i * tq
    s = k_ref.shape[1]
    dh = HEAD_DIM
    kv_heads = k_ref.shape[2] // dh
    n_heads = q_ref.shape[2] // dh
    rep = n_heads // kv_heads
    n_idx_heads = qi_ref.shape[2] // IDX_DIM

    @pl.when(i == 0)
    def _():
        def prep_chunk(c, carry):
            c0 = pl.multiple_of(c * TR_CHUNK, TR_CHUNK)
            kblk = k_ref[0, pl.ds(c0, TR_CHUNK), :].astype(F32)
            vblk_t = v_ref[0, pl.ds(c0, TR_CHUNK), :].astype(F32).T
            for g in range(kv_heads):
                kext_ref[g, pl.ds(c0, TR_CHUNK), 0:dh] = kblk[:, g * dh:(g + 1) * dh].astype(BF16)
                kext_ref[g, pl.ds(c0, TR_CHUNK), dh:2 * dh] = kcols_ref[pl.ds(c0, TR_CHUNK), :]
                vt_ref[g, 0:dh, pl.ds(c0, TR_CHUNK)] = vblk_t[g * dh:(g + 1) * dh].astype(BF16)
            return carry
        lax.fori_loop(0, s // TR_CHUNK, prep_chunk, 0)
        ones_row = lax.broadcasted_iota(I32, (BF16_SUBLANES, s), 0) == 0
        for g in range(kv_heads):
            vt_ref[g, dh:dh + BF16_SUBLANES, :] = jnp.where(ones_row, 1.0, 0.0).astype(BF16)

    qf = q_ref[0].astype(F32) * (QK_SCALE * LOG2E)
    for h in range(n_heads):
        qs_ref[h, :, 0:dh] = qf[:, h * dh:(h + 1) * dh].astype(BF16)
        qs_ref[h, :, dh:2 * dh] = jnp.broadcast_to(qcols_ref[h], (tq, dh))
    qif = qi_ref[0].astype(F32)
    lane = lax.broadcasted_iota(I32, (tq, LANES), 1)
    per_tile = LANES // IDX_DIM
    for h in range(n_idx_heads):
        tile = qif[:, (h // per_tile) * LANES:(h // per_tile + 1) * LANES]
        shift = (h % per_tile) * IDX_DIM
        if shift:
            tile = pltpu.roll(tile, LANES - shift, 1)
        qie_ref[h] = jnp.where(lane < IDX_DIM, tile, 0.0).astype(BF16)
    wt = kiwq_ref[0].astype(F32).T[IDX_DIM:IDX_DIM + n_idx_heads] * IDX_SCALE

    nsb = (t0 + tq + tkb - 1) // tkb
    srow = lax.broadcasted_iota(I32, (tkb, tq), 0)
    tcol = t0 + lax.broadcasted_iota(I32, (tkb, tq), 1)

    def score_body(jb, carry):
        s0 = pl.multiple_of(jb * tkb, tkb)
        kib = kiw_ref[0, pl.ds(s0, tkb), :]
        sc = jnp.zeros((tkb, tq), F32)
        for h in range(n_idx_heads):
            z = lax.dot_general(kib, qie_ref[h], _NT, preferred_element_type=F32)
            sc += wt[h:h + 1, :] * jnp.maximum(z, 0.0)
        sc_ref[pl.ds(s0, tkb), :] = jnp.where(s0 + srow <= tcol, sc, -jnp.inf)
        return carry

    lax.fori_loop(0, nsb, score_body, 0)

    def count(pred):
        def body(jb, acc):
            s0 = pl.multiple_of(jb * tkb, tkb)
            ind = jnp.where(pred(sc_ref[pl.ds(s0, tkb), :], s0 + srow), 1.0, 0.0)
            return acc + jnp.sum(ind.reshape(tkb // COUNT_ROWS, COUNT_ROWS, tq), axis=0)
        acc = lax.fori_loop(0, nsb, body, jnp.zeros((COUNT_ROWS, tq), F32))
        return jnp.sum(acc, axis=0, keepdims=True)

    kf = float(topk)
    c0 = count(lambda sc, sidx: sc >= 0.0)
    ok0 = c0 >= kf
    thr_key0 = jnp.where(ok0, jnp.int32(0), jnp.int32(INT_MIN))
    cnt0 = jnp.where(ok0, c0, 0.0)

    def thr_cond(carry):
        it, _, cnt = carry
        return jnp.logical_and(it < 31, jnp.max(jnp.abs(cnt - kf)) > 0.0)

    def bit_body(it, carry):
        thr_key, cnt = carry
        cand = thr_key | (jnp.int32(1) << (30 - it))
        cand_f = _key_to_float(cand)
        c = count(lambda sc, sidx: sc >= cand_f)
        ok = c >= kf
        return jnp.where(ok, cand, thr_key), jnp.where(ok, c, cnt)

    def thr_body(carry):
        it, thr_key, cnt = carry
        nxt = jnp.minimum(it + BITS_PER_CHECK, 31)
        thr_key, cnt = lax.fori_loop(it, nxt, bit_body, (thr_key, cnt))
        return nxt, thr_key, cnt

    _, thr_key, n_ge = lax.while_loop(thr_cond, thr_body, (jnp.int32(0), thr_key0, cnt0))
    thr = jnp.where(thr_key == INT_MIN, -jnp.inf, _key_to_float(thr_key))
    has_ties = jnp.max(n_ge) > kf
    tie_passes = jnp.where(has_ties, 1, 0)
    n_gt = lax.fori_loop(0, tie_passes, lambda it, c: count(lambda sc, sidx: sc > thr), jnp.zeros((1, tq), F32))
    need = kf - n_gt

    def cut_body(it, cut):
        cand = cut | (jnp.int32(1) << (12 - it))
        c = count(lambda sc, sidx: jnp.logical_and(sc == thr, sidx < cand))
        return jnp.where(c <= need, cand, cut)

    cut = lax.fori_loop(0, jnp.where(has_ties, 13, 0), cut_body, jnp.zeros((1, tq), I32))
    cut = jnp.where(has_ties, cut, jnp.int32(2 ** 30))

    nkb = (t0 + tq + tk - 1) // tk
    srow_k = lax.broadcasted_iota(I32, (tk, tq), 0)
    tcol_k = t0 + lax.broadcasted_iota(I32, (tk, tq), 1)

    def set_block_bias(jb):
        s0 = pl.multiple_of(jb * tk, tk)
        sc = sc_ref[pl.ds(s0, tk), :]
        sidx = s0 + srow_k
        sel = jnp.logical_or(sc > thr, jnp.logical_and(sc == thr, sidx < cut))
        sel = jnp.logical_and(sel, sidx <= tcol_k)
        bias_ref[...] = jnp.where(sel, 0.0, NEG_BIG)
        return s0

    def logits(h, kb):
        return lax.dot_general(kb, qs_ref[h], _NT, preferred_element_type=F32)

    m_ref[...] = jnp.full(m_ref.shape, NEG_BIG, F32)
    acc_ref[...] = jnp.zeros_like(acc_ref)

    def attn_body(jb, carry):
        s0 = set_block_bias(jb)
        maxima = []
        for g in range(kv_heads):
            kb = kext_ref[g, pl.ds(s0, tk), :]
            for r in range(rep):
                h = g * rep + r
                x = logits(h, kb) + bias_ref[...]
                xs_ref[h] = x
                maxima.append(jnp.max(x, axis=0, keepdims=True))
        for g in range(kv_heads):
            vt = vt_ref[g, :, pl.ds(s0, tk)]
            for r in range(rep):
                h = g * rep + r
                m_prev = m_ref[h]
                m_new = jnp.maximum(m_prev, maxima[h])
                p = jnp.exp2(xs_ref[h] - m_new).astype(BF16)
                acc_ref[h] = jnp.exp2(m_prev - m_new) * acc_ref[h] + jnp.dot(
                    vt, p, preferred_element_type=F32)
                m_ref[h] = m_new
        return carry

    lax.fori_loop(0, nkb, attn_body, 0)
    for p in range(n_heads // 2):
        outs = []
        for h in (2 * p, 2 * p + 1):
            acc = acc_ref[h]
            outs.append(acc[:dh] / acc[dh:dh + 1])
        o_ref[0, :, p * LANES:(p + 1) * LANES] = jnp.concatenate(outs, axis=0).T.astype(o_ref.dtype)


def _dsa_attention(proj, topk, tq=256, tkb=512, tk=256):
    b, s, n = proj.shape
    dh = HEAD_DIM
    qw = DSA_HEADS * dh
    kvw = DSA_KV_HEADS * dh
    iw = IDX_HEADS * IDX_DIM
    assert qw % kvw == 0 and (qw + 2 * kvw) % iw == 0 and (qw + 2 * kvw + iw) % LANES == 0
    assert IDX_DIM + IDX_HEADS <= LANES and n == qw + 2 * kvw + iw + LANES
    q_cols, k_cols = _alibi_columns(s)
    tkb = min(tkb, s)
    tk = min(tk, s)
    kiw_block = (qw + 2 * kvw + iw) // LANES
    dv = dh + BF16_SUBLANES
    return pl.pallas_call(
        functools.partial(_dsa_kernel, tq=tq, tkb=tkb, tk=tk, topk=topk),
        grid=(b, s // tq),
        in_specs=[pl.BlockSpec((1, tq, qw), lambda bi, i: (bi, i, 0)),
                  pl.BlockSpec((1, s, kvw), lambda bi, i: (bi, 0, qw // kvw)),
                  pl.BlockSpec((1, s, kvw), lambda bi, i: (bi, 0, qw // kvw + 1)),
                  pl.BlockSpec((1, tq, iw), lambda bi, i: (bi, i, (qw + 2 * kvw) // iw)),
                  pl.BlockSpec((1, s, LANES), lambda bi, i: (bi, 0, kiw_block)),
                  pl.BlockSpec((1, tq, LANES), lambda bi, i: (bi, i, kiw_block)),
                  pl.BlockSpec((DSA_HEADS, 1, dh), lambda bi, i: (0, 0, 0)),
                  pl.BlockSpec((s, dh), lambda bi, i: (0, 0))],
        out_specs=pl.BlockSpec((1, tq, qw), lambda bi, i: (bi, i, 0)),
        out_shape=jax.ShapeDtypeStruct((b, s, qw), BF16),
        scratch_shapes=[pltpu.VMEM((DSA_KV_HEADS, s, 2 * dh), BF16),
                        pltpu.VMEM((DSA_KV_HEADS, dv, s), BF16),
                        pltpu.VMEM((IDX_HEADS, tq, LANES), BF16),
                        pltpu.VMEM((s, tq), F32),
                        pltpu.VMEM((DSA_HEADS, 1, tq), F32),
                        pltpu.VMEM((DSA_HEADS, dv, tq), F32),
                        pltpu.VMEM((DSA_HEADS, tq, 2 * dh), BF16),
                        pltpu.VMEM((tk, tq), F32),
                        pltpu.VMEM((DSA_HEADS, tk, tq), F32)],
        compiler_params=_params("parallel", "arbitrary"),
        name="dsa_attention",
    )(proj, proj, proj, proj, proj, proj, q_cols, k_cols)


def _row(v):
    return v.astype(F32)[None]


def _even_layer(x2d, bsz, s, w_in, ssm_ops, w_glu, b_glu, w_out, g_mix, b_mix, mlp):
    proj = _proj(x2d, w_in.astype(BF16)).reshape(bsz, s, -1)
    a_out = _sb_attention(proj).reshape(bsz * s, SB_WIDTH)
    y = _ssm_scan(proj, *ssm_ops, nch=s // (SSM_CHUNK * SUBLANES), nseg=SUBLANES).reshape(bsz * s, SSM_WIDTH)
    consts = (w_glu.astype(BF16), _row(b_glu), w_out.astype(BF16), _row(g_mix), _row(b_mix)) + mlp
    return _layer_tail(_tail_even_kernel, "tail_even", (a_out, y, x2d), consts)


def _odd_layer(x2d, bsz, s, w_in, w_out, g_mix, b_mix, mlp):
    proj = _proj(x2d, w_in.astype(BF16)).reshape(bsz, s, -1)
    o = _dsa_attention(proj, min(TOPK_MAX, s // 4)).reshape(bsz * s, DSA_HEADS * HEAD_DIM)
    consts = (w_out.astype(BF16), _row(g_mix), _row(b_mix)) + mlp
    return _layer_tail(_tail_odd_kernel, "tail_odd", (o, x2d), consts)


def kernel(x, sb_ssm_w_in, ssm_log_dt, ssm_lam_re, ssm_lam_im, ssm_b_re, ssm_b_im, ssm_c_re, ssm_c_im, ssm_d,
           ssm_w_glu, ssm_b_glu, sb_ssm_w_out, dsa_w_in, dsa_w_out, ln_mix_g, ln_mix_b, ln_ffn_g, ln_ffn_b,
           mlp_w1, mlp_w2):
    bsz, s, d = x.shape
    x2d = x.reshape(bsz * s, d).astype(F32)
    depth = mlp_w1.shape[0]
    stacked = [p.reshape((-1,) + p.shape[2:]) for p in
               (ssm_log_dt, ssm_lam_re, ssm_lam_im, ssm_b_re, ssm_b_im, ssm_c_re, ssm_c_im, ssm_d)]
    ssm_ops = _ssm_operators(*stacked, nch=s // (SSM_CHUNK * SUBLANES))
    tiles = SSM_WIDTH // LANES
    for i in range(depth):
        j = i // 2
        mlp = (mlp_w1[i].astype(BF16), mlp_w2[i].astype(BF16), _row(ln_ffn_g[i]), _row(ln_ffn_b[i]))
        if i % 2 == 0:
            layer_ops = tuple(op[j * tiles:(j + 1) * tiles] for op in ssm_ops)
            x2d = _even_layer(x2d, bsz, s, sb_ssm_w_in[j], layer_ops, ssm_w_glu[j], ssm_b_glu[j],
                              sb_ssm_w_out[j], ln_mix_g[i], ln_mix_b[i], mlp)
        else:
            n_in = dsa_w_in.shape[2]
            pad = (-n_in) % LANES
            w_in = jnp.pad(dsa_w_in[j], ((0, 0), (0, pad)))
            x2d = _odd_layer(x2d, bsz, s, w_in, dsa_w_out[j], ln_mix_g[i], ln_mix_b[i], mlp)
    return x2d.reshape(bsz, s, d).astype(x.dtype)
```

```python
import functools
import math

import jax
import jax.numpy as jnp
import numpy as np
from jax import lax
from jax.experimental import pallas as pl
from jax.experimental.pallas import tpu as pltpu

F32 = jnp.float32
BF16 = jnp.bfloat16
I32 = jnp.int32

HEAD_DIM = 64
SB_WIDTH = 512
SB_HEADS = SB_WIDTH // HEAD_DIM
SSM_WIDTH = 512
SSM_GROUP = 16
SSM_STATE = 64
DSA_HEADS = 16
DSA_KV_HEADS = 4
IDX_HEADS = 8
IDX_DIM = 32
TOPK_MAX = 256
DEPTH = 4
ALPHA = (2 * DEPTH) ** 0.25
LN_EPS = 1e-5
IDX_SCALE = (IDX_DIM ** -0.5) * (IDX_HEADS ** -0.5)
QK_SCALE = HEAD_DIM ** -0.5

SSM_CHUNK = 16
SUBLANES = 8
BF16_SUBLANES = 16
LANES = 128
VMEM_LIMIT = 56 * 1024 * 1024
EXP2_UNDERFLOW = -150.0
NEG_BIG = -1e30
INT_MIN = -2 ** 31
COUNT_ROWS = 64
TR_CHUNK = 256
LOG2E = math.log2(math.e)
MLP_FCHUNK = 1024
BITS_PER_CHECK = 8

_NT = (((1,), (1,)), ((), ()))


def _params(*sem):
    return pltpu.CompilerParams(dimension_semantics=sem, vmem_limit_bytes=VMEM_LIMIT)


def _layer_norm(r, g, b):
    mu = jnp.mean(r, axis=-1, keepdims=True)
    c = r - mu
    var = jnp.mean(c * c, axis=-1, keepdims=True)
    return c * lax.rsqrt(var + LN_EPS) * g + b


def _proj_kernel(x_ref, w_ref, o_ref):
    o_ref[...] = jnp.dot(x_ref[...].astype(BF16), w_ref[...],
                         preferred_element_type=F32).astype(o_ref.dtype)


def _proj(x2d, w, tm=512):
    m, k = x2d.shape
    n = w.shape[1]
    return pl.pallas_call(
        _proj_kernel,
        grid=(m // tm,),
        in_specs=[pl.BlockSpec((tm, k), lambda i: (i, 0)),
                  pl.BlockSpec((k, n), lambda i: (0, 0))],
        out_specs=pl.BlockSpec((tm, n), lambda i: (i, 0)),
        out_shape=jax.ShapeDtypeStruct((m, n), BF16),
        compiler_params=_params("parallel"),
        name="proj",
    )(x2d, w)


def _sb_kernel(q_ref, k_ref, v_ref, o_ref, vt_ref, qm_ref, acc_ref, a_ref, *, tq, tk):
    i = pl.program_id(1)
    s = k_ref.shape[1]
    dh = HEAD_DIM
    heads = q_ref.shape[2] // dh
    pairs = heads // 2

    @pl.when(i == 0)
    def _():
        def transpose_chunk(c, carry):
            c0 = pl.multiple_of(c * TR_CHUNK, TR_CHUNK)
            vt_ref[:, pl.ds(c0, TR_CHUNK)] = v_ref[0, pl.ds(c0, TR_CHUNK), :].astype(F32).T.astype(BF16)
            return carry
        lax.fori_loop(0, s // TR_CHUNK, transpose_chunk, 0)

    lane = lax.broadcasted_iota(I32, (tq, LANES), 1)
    for p in range(pairs):
        qp = q_ref[0, :, p * LANES:(p + 1) * LANES].astype(F32) * (QK_SCALE * LOG2E)
        qm_ref[2 * p] = jnp.where(lane < dh, qp, 0.0).astype(BF16)
        qm_ref[2 * p + 1] = jnp.where(lane >= dh, qp, 0.0).astype(BF16)
    acc_ref[...] = jnp.zeros_like(acc_ref)
    a_ref[...] = jnp.zeros_like(a_ref)
    rows = lax.broadcasted_iota(I32, (tk, tq), 0)
    tcol = i * tq + lax.broadcasted_iota(I32, (tk, tq), 1)
    tri = lax.broadcasted_iota(I32, (tk, tk), 1) > lax.broadcasted_iota(I32, (tk, tk), 0)
    tri = jnp.where(tri, 1.0, 0.0).astype(BF16)
    tri2 = jnp.concatenate([tri, tri], axis=1)

    def body(carry):
        j, _ = carry
        k0 = pl.multiple_of(j * tk, tk)
        mask = (k0 + rows) < tcol
        hs = range(heads)
        kbs = [k_ref[0, pl.ds(k0, tk), p * LANES:(p + 1) * LANES] for p in range(pairs)]
        zs = [lax.dot_general(kbs[h // 2], qm_ref[h], _NT, preferred_element_type=F32) for h in hs]
        sps = [jnp.log2(1.0 + jnp.exp2(-jnp.abs(z))) for z in zs]
        log_betas = [jnp.minimum(z, 0.0) - sp for z, sp in zip(zs, sps)]
        l1s = [jnp.where(mask, -jnp.maximum(z, 0.0) - sp, 0.0) for z, sp in zip(zs, sps)]
        his = [l1.astype(BF16) for l1 in l1s]
        los = [(l1 - hi.astype(F32)).astype(BF16) for l1, hi in zip(l1s, his)]
        es = [jnp.dot(tri2, jnp.concatenate([hi, lo], axis=0), preferred_element_type=F32)
              for hi, lo in zip(his, los)]
        a_olds = [a_ref[h] for h in hs]
        ws = [jnp.where(mask, jnp.exp2(lb + a + e), 0.0).astype(BF16) for lb, a, e in zip(log_betas, a_olds, es)]
        amax = None
        for h in hs:
            acc_ref[h] += jnp.dot(vt_ref[h * dh:(h + 1) * dh, pl.ds(k0, tk)], ws[h], preferred_element_type=F32)
            a_new = a_olds[h] + jnp.sum(l1s[h], axis=0, keepdims=True)
            a_ref[h] = a_new
            amax = a_new if amax is None else jnp.maximum(amax, a_new)
        return j - 1, jnp.max(amax) > EXP2_UNDERFLOW

    j0 = ((i + 1) * tq - 1) // tk
    lax.while_loop(lambda c: jnp.logical_and(c[0] >= 0, c[1]), body, (j0, True))
    for p in range(pairs):
        both = jnp.concatenate([acc_ref[2 * p], acc_ref[2 * p + 1]], axis=0)
        o_ref[0, :, p * LANES:(p + 1) * LANES] = both.T.astype(o_ref.dtype)


def _sb_attention(proj, tq=256, tk=128):
    b, s, _ = proj.shape
    w = SB_WIDTH
    return pl.pallas_call(
        functools.partial(_sb_kernel, tq=tq, tk=tk),
        grid=(b, s // tq),
        in_specs=[pl.BlockSpec((1, tq, w), lambda bi, i: (bi, i, 0)),
                  pl.BlockSpec((1, s, w), lambda bi, i: (bi, 0, 1)),
                  pl.BlockSpec((1, s, w), lambda bi, i: (bi, 0, 2))],
        out_specs=pl.BlockSpec((1, tq, w), lambda bi, i: (bi, i, 0)),
        out_shape=jax.ShapeDtypeStruct((b, s, w), BF16),
        scratch_shapes=[pltpu.VMEM((w, s), BF16),
                        pltpu.VMEM((SB_HEADS, tq, LANES), BF16),
                        pltpu.VMEM((SB_HEADS, HEAD_DIM, tq), F32),
                        pltpu.VMEM((SB_HEADS, 1, tq), F32)],
        compiler_params=_params("parallel", "arbitrary"),
        name="sb_attention",
    )(proj, proj, proj)


def _ssm_kmat_kernel(l_ref, b_ref, o_ref):
    for g in range(l_ref.shape[0]):
        o_ref[g] = jnp.dot(l_ref[g], b_ref[g], precision=lax.Precision.HIGHEST,
                           preferred_element_type=F32)


def _ssm_kmat(lcat, bcat, gb=8):
    g, r, n2 = lcat.shape
    c = bcat.shape[2]
    return pl.pallas_call(
        _ssm_kmat_kernel,
        grid=(g // gb,),
        in_specs=[pl.BlockSpec((gb, r, n2), lambda i: (i, 0, 0)),
                  pl.BlockSpec((gb, n2, c), lambda i: (i, 0, 0))],
        out_specs=pl.BlockSpec((gb, r, c), lambda i: (i, 0, 0)),
        out_shape=jax.ShapeDtypeStruct((g, r, c), F32),
        compiler_params=_params("parallel"),
        name="ssm_kmat",
    )(lcat, bcat)


def _ssm_expand_kernel(e_ref, k_ref, qr_ref, qi_ref, rr_ref, ri_ref, wk_o, qr_o, qi_o, rr_o, ri_o):
    C, N, gt = SSM_GROUP, SSM_STATE, LANES // SSM_GROUP
    e = e_ref[...]

    def rows_out(x, lanes_per_group):
        y = jnp.dot(e, x, preferred_element_type=F32)
        row_group = (lax.broadcasted_iota(I32, y.shape, 0) // C) % gt
        lane_group = lax.broadcasted_iota(I32, y.shape, 1) // lanes_per_group
        return jnp.where(row_group == lane_group, y, 0.0).astype(BF16)

    def lanes_out(x):
        y = lax.dot_general(x, e, _NT, preferred_element_type=F32)
        row_group = lax.broadcasted_iota(I32, y.shape, 0) // N
        lane_group = (lax.broadcasted_iota(I32, y.shape, 1) // C) % gt
        return jnp.where(row_group == lane_group, y, 0.0).astype(BF16)

    wk_o[0] = rows_out(k_ref[0], C)
    qr_o[0] = rows_out(qr_ref[0], N)
    qi_o[0] = rows_out(qi_ref[0], N)
    rr_o[0] = lanes_out(rr_ref[0])
    ri_o[0] = lanes_out(ri_ref[0])


def _ssm_expand(kcat, qr, qi, rr, ri):
    tiles, lc, _ = kcat.shape
    L, C = SSM_CHUNK, SSM_GROUP
    gt = LANES // C
    w = L * LANES
    ns = qr.shape[2]
    place = np.zeros((L, gt, C, L, C), np.float32)
    for s in range(L):
        for c in range(C):
            place[s, :, c, s, c] = 1.0
    e = jnp.asarray(place.reshape(w, lc), BF16)
    per_tile = lambda a: pl.BlockSpec((1,) + a.shape[1:], lambda j: (j, 0, 0))
    out = lambda r, c: (jax.ShapeDtypeStruct((tiles, r, c), BF16), pl.BlockSpec((1, r, c), lambda j: (j, 0, 0)))
    outs = [out(w, LANES), out(w, ns), out(w, ns), out(ns, w), out(ns, w)]
    return pl.pallas_call(
        _ssm_expand_kernel,
        grid=(tiles,),
        in_specs=[pl.BlockSpec(e.shape, lambda j: (0, 0))] + [per_tile(a) for a in (kcat, qr, qi, rr, ri)],
        out_specs=[o[1] for o in outs],
        out_shape=[o[0] for o in outs],
        compiler_params=_params("parallel"),
        name="ssm_expand",
    )(e, kcat, qr, qi, rr, ri)


def _cmul(ar, ai, xr, xi):
    return ar * xr - ai * xi, ar * xi + ai * xr


def _ssm_kernel(u_ref, wk_ref, qr_ref, qi_ref, rr_ref, ri_ref, al_ref, aseg_ref, d_ref, y_ref,
                uf_ref, uc_ref, xr_ref, xi_ref, *, nch, nseg):
    L = SSM_CHUNK
    seg_rows = nch * L
    rows = nch * nseg
    uf_ref[...] = u_ref[0].astype(F32)
    for c in range(nch):
        for s in range(L):
            uc_ref[c * nseg:(c + 1) * nseg, s * LANES:(s + 1) * LANES] = \
                uf_ref[pl.ds(c * L + s, nseg, stride=seg_rows), :]
    ucb = uc_ref[...].astype(BF16)
    xr_ref[...] = jnp.dot(ucb, qr_ref[0], preferred_element_type=F32)
    xi_ref[...] = jnp.dot(ucb, qi_ref[0], preferred_element_type=F32)
    alr, ali = al_ref[0, 0:1, :], al_ref[0, 1:2, :]

    def scan_body(c, carry):
        sr, si = carry
        r0 = pl.multiple_of(c * nseg, nseg)
        inr = xr_ref[pl.ds(r0, nseg), :]
        ini = xi_ref[pl.ds(r0, nseg), :]
        xr_ref[pl.ds(r0, nseg), :] = sr
        xi_ref[pl.ds(r0, nseg), :] = si
        pr, pi = _cmul(alr, ali, sr, si)
        return pr + inr, pi + ini

    zero = jnp.zeros((nseg, xr_ref.shape[1]), F32)
    er, ei = lax.fori_loop(0, nch, scan_body, (zero, zero))

    row = lax.broadcasted_iota(I32, zero.shape, 0)
    cr, ci = zero, zero
    for _ in range(nseg - 1):
        pr, pi = _cmul(aseg_ref[0, 0:1, :], aseg_ref[0, 1:2, :], cr, ci)
        cr = jnp.where(row >= 1, pltpu.roll(er + pr, 1, 0), 0.0)
        ci = jnp.where(row >= 1, pltpu.roll(ei + pi, 1, 0), 0.0)

    def corr_body(c, carry):
        cr, ci = carry
        r0 = pl.multiple_of(c * nseg, nseg)
        xr_ref[pl.ds(r0, nseg), :] += cr
        xi_ref[pl.ds(r0, nseg), :] += ci
        return _cmul(alr, ali, cr, ci)

    lax.fori_loop(0, nch, corr_body, (cr, ci))

    yc = jnp.dot(xr_ref[...].astype(BF16), rr_ref[0], preferred_element_type=F32)
    yc += jnp.dot(xi_ref[...].astype(BF16), ri_ref[0], preferred_element_type=F32)
    for t in range(L):
        lanes = slice(t * LANES, (t + 1) * LANES)
        y_t = jnp.dot(ucb[:, :(t + 1) * LANES], wk_ref[0, (L - 1 - t) * LANES:, :], preferred_element_type=F32)
        y_t += yc[:, lanes] + uc_ref[:, lanes] * d_ref[0]
        for c in range(nch):
            y_ref[0, pl.ds(c * L + t, nseg, stride=seg_rows), :] = y_t[c * nseg:(c + 1) * nseg]


def _ssm_scan(proj, wk, qr, qi, rr, ri, al, aseg, d_t, *, nch, nseg):
    b, s, n = proj.shape
    tiles = SSM_WIDTH // LANES
    first = (n - SSM_WIDTH) // LANES
    rows = nch * nseg
    w = SSM_CHUNK * LANES
    ns = qr.shape[2]
    per_tile = lambda r, c: pl.BlockSpec((1, r, c), lambda j, bi: (j, 0, 0))
    return pl.pallas_call(
        functools.partial(_ssm_kernel, nch=nch, nseg=nseg),
        grid=(tiles, b),
        in_specs=[pl.BlockSpec((1, s, LANES), lambda j, bi: (bi, 0, first + j)),
                  per_tile(w, LANES), per_tile(w, ns), per_tile(w, ns), per_tile(ns, w), per_tile(ns, w),
                  per_tile(2, ns), per_tile(2, ns), per_tile(1, LANES)],
        out_specs=pl.BlockSpec((1, s, LANES), lambda j, bi: (bi, 0, j)),
        out_shape=jax.ShapeDtypeStruct((b, s, SSM_WIDTH), F32),
        scratch_shapes=[pltpu.VMEM((s, LANES), F32), pltpu.VMEM((rows, w), F32),
                        pltpu.VMEM((rows, ns), F32), pltpu.VMEM((rows, ns), F32)],
        compiler_params=_params("parallel", "parallel"),
        name="ssm_scan",
    )(proj, wk, qr, qi, rr, ri, al, aseg, d_t)


def _ssm_operators(log_dt, lam_re, lam_im, b_re, b_im, c_re, c_im, d, nch):
    L, C, N = SSM_CHUNK, SSM_GROUP, SSM_STATE
    G = log_dt.shape[0]
    dt = jnp.exp(log_dt.astype(F32))[:, None]
    lr = lam_re.astype(F32)
    li = lam_im.astype(F32)

    def apow(tau):
        tau = jnp.asarray(tau, F32)[..., None, None]
        mag = jnp.exp(tau * (lr * dt))
        ang = tau * (li * dt)
        return mag * jnp.cos(ang), mag * jnp.sin(ang)

    pr, pi = apow(np.arange(L + 1))
    den = lr * lr + li * li
    nr = pr[1] - 1.0
    coef_r = ((nr * lr + pi[1] * li) / den)[..., None]
    coef_i = ((pi[1] * lr - nr * li) / den)[..., None]
    br = b_re.astype(F32)
    bi = b_im.astype(F32)
    bbar_r = coef_r * br - coef_i * bi
    bbar_i = coef_r * bi + coef_i * br
    cr = c_re.astype(F32)
    ci = c_im.astype(F32)
    car = cr[None] * pr[:, :, None, :] - ci[None] * pi[:, :, None, :]
    cai = cr[None] * pi[:, :, None, :] + ci[None] * pr[:, :, None, :]
    lcat = jnp.concatenate([car[:L], -cai[:L]], axis=-1)
    lcat = lcat.transpose(1, 0, 2, 3).reshape(G, L * C, 2 * N)
    bcat = jnp.concatenate([bbar_r, bbar_i], axis=1)
    kmat = _ssm_kmat(lcat, bcat).reshape(G, L, C, C)
    gt = LANES // C
    tiles = G // gt
    kcat = kmat.reshape(tiles, gt, L, C, C)[:, :, ::-1].transpose(0, 2, 4, 1, 3).reshape(tiles, L * C, LANES)
    prq = pr[L - 1 - np.arange(L)]
    piq = pi[L - 1 - np.arange(L)]
    qr = prq[..., None] * bbar_r[None] - piq[..., None] * bbar_i[None]
    qi = prq[..., None] * bbar_i[None] + piq[..., None] * bbar_r[None]
    cat_q = lambda q: q.reshape(L, tiles, gt, N, C).transpose(1, 0, 4, 2, 3).reshape(tiles, L * C, gt * N)
    cat_r = lambda r: r.reshape(L, tiles, gt, C, N).transpose(1, 2, 4, 0, 3).reshape(tiles, gt * N, L * C)
    wk, qr, qi, rr, ri = _ssm_expand(*(a.astype(BF16) for a in
                                       (kcat, cat_q(qr), cat_q(qi), cat_r(car[1:]), cat_r(-cai[1:]))))
    sr, si = apow(np.asarray([L * nch]))
    al = jnp.stack([pr[L], pi[L]], axis=1).reshape(tiles, gt, 2, N).transpose(0, 2, 1, 3).reshape(tiles, 2, gt * N)
    aseg = jnp.stack([sr[0], si[0]], axis=1).reshape(tiles, gt, 2, N).transpose(0, 2, 1, 3).reshape(tiles, 2, gt * N)
    d_t = d.astype(F32).reshape(tiles, 1, LANES)
    return wk, qr, qi, rr, ri, al, aseg, d_t


def _mlp_ln(x, w1_ref, w2_ref, g_ref, b_ref):
    xb = x.astype(BF16)
    acc = jnp.zeros(x.shape, F32)
    for f in range(0, w1_ref.shape[1], MLP_FCHUNK):
        h = jnp.dot(xb, w1_ref[:, f:f + MLP_FCHUNK], preferred_element_type=F32)
        h = jnp.maximum(h, 0.0)
        acc += jnp.dot((h * h).astype(BF16), w2_ref[f:f + MLP_FCHUNK, :], preferred_element_type=F32)
    return _layer_norm(ALPHA * x + acc, g_ref[...], b_ref[...])


def _tail_even_kernel(a_ref, y_ref, x_ref, wglu_ref, bglu_ref, wo_ref, g1_ref, b1_ref,
                      w1_ref, w2_ref, g2_ref, b2_ref, o_ref):
    y = jax.nn.gelu(y_ref[...], approximate=True)
    gate = jnp.dot(y.astype(BF16), wglu_ref[...], preferred_element_type=F32) + bglu_ref[...]
    y = y * (1.0 / (1.0 + jnp.exp(-gate)))
    h = jnp.dot(a_ref[...], wo_ref[:SB_WIDTH, :], preferred_element_type=F32)
    h += jnp.dot(y.astype(BF16), wo_ref[SB_WIDTH:, :], preferred_element_type=F32)
    x1 = _layer_norm(ALPHA * x_ref[...] + h, g1_ref[...], b1_ref[...])
    o_ref[...] = _mlp_ln(x1, w1_ref, w2_ref, g2_ref, b2_ref)


def _tail_odd_kernel(o_in_ref, x_ref, wo_ref, g1_ref, b1_ref, w1_ref, w2_ref, g2_ref, b2_ref, o_ref):
    h = jnp.dot(o_in_ref[...], wo_ref[...], preferred_element_type=F32)
    x1 = _layer_norm(ALPHA * x_ref[...] + h, g1_ref[...], b1_ref[...])
    o_ref[...] = _mlp_ln(x1, w1_ref, w2_ref, g2_ref, b2_ref)


def _layer_tail(body, name, rows, consts, tm=512):
    m, d = rows[-1].shape
    row = lambda a: pl.BlockSpec((tm, a.shape[1]), lambda i: (i, 0))
    once = lambda a: pl.BlockSpec(a.shape, lambda i: (0, 0), pipeline_mode=pl.Buffered(1))
    return pl.pallas_call(
        body,
        grid=(m // tm,),
        in_specs=[row(a) for a in rows] + [once(a) for a in consts],
        out_specs=pl.BlockSpec((tm, d), lambda i: (i, 0)),
        out_shape=jax.ShapeDtypeStruct((m, d), F32),
        compiler_params=_params("parallel"),
        name=name,
    )(*rows, *consts)


def _bf16_split3(x):
    x = np.asarray(x, np.float32)
    hi = x.astype(BF16).astype(np.float32)
    mid = (x - hi).astype(BF16).astype(np.float32)
    lo = (x - hi - mid).astype(BF16).astype(np.float32)
    return hi, mid, lo


def _alibi_columns(s):
    slopes = (2.0 ** (-8.0 * np.arange(1, DSA_HEADS + 1) / DSA_HEADS)).astype(np.float32)
    slopes = (slopes.astype(np.float64) * LOG2E).astype(np.float32)
    parts = _bf16_split3(slopes)
    assert np.all(parts[0] + parts[1] + parts[2] == slopes)
    q_cols = np.zeros((DSA_HEADS, 1, HEAD_DIM), np.float32)
    q_cols[:, 0, :6] = np.stack(parts * 2, axis=-1)
    pos = np.arange(s)
    k_cols = np.zeros((s, HEAD_DIM), np.float32)
    k_cols[:, 0:3] = (pos // 64 * 64)[:, None]
    k_cols[:, 3:6] = (pos % 64)[:, None]
    return jnp.asarray(q_cols, BF16), jnp.asarray(k_cols, BF16)


def _key_to_float(key):
    bits = key ^ ((key >> 31) & jnp.int32(0x7FFFFFFF))
    return lax.bitcast_convert_type(bits, F32)


def _dsa_kernel(q_ref, k_ref, v_ref, qi_ref, kiw_ref, kiwq_ref, qcols_ref, kcols_ref, o_ref,
                kext_ref, vt_ref, qie_ref, sc_ref, m_ref, acc_ref, qs_ref, bias_ref, xs_ref,
                *, tq, tkb, tk, topk):
    i = pl.program_id(1)
    t0 = i * tq
    s = k_ref.shape[1]
    dh = HEAD_DIM
    kv_heads = k_ref.shape[2] // dh
    n_heads = q_ref.shape[2] // dh
    rep = n_heads // kv_heads
    n_idx_heads = qi_ref.shape[2] // IDX_DIM

    @pl.when(i == 0)
    def _():
        def prep_chunk(c, carry):
            c0 = pl.multiple_of(c * TR_CHUNK, TR_CHUNK)
            kblk = k_ref[0, pl.ds(c0, TR_CHUNK), :].astype(F32)
            vblk_t = v_ref[0, pl.ds(c0, TR_CHUNK), :].astype(F32).T
            for g in range(kv_heads):
                kext_ref[g, pl.ds(c0, TR_CHUNK), 0:dh] = kblk[:, g * dh:(g + 1) * dh].astype(BF16)
                kext_ref[g, pl.ds(c0, TR_CHUNK), dh:2 * dh] = kcols_ref[pl.ds(c0, TR_CHUNK), :]
                vt_ref[g, 0:dh, pl.ds(c0, TR_CHUNK)] = vblk_t[g * dh:(g + 1) * dh].astype(BF16)
            return carry
        lax.fori_loop(0, s // TR_CHUNK, prep_chunk, 0)
        ones_row = lax.broadcasted_iota(I32, (BF16_SUBLANES, s), 0) == 0
        for g in range(kv_heads):
            vt_ref[g, dh:dh + BF16_SUBLANES, :] = jnp.where(ones_row, 1.0, 0.0).astype(BF16)

    qf = q_ref[0].astype(F32) * (QK_SCALE * LOG2E)
    for h in range(n_heads):
        qs_ref[h, :, 0:dh] = qf[:, h * dh:(h + 1) * dh].astype(BF16)
        qs_ref[h, :, dh:2 * dh] = jnp.broadcast_to(qcols_ref[h], (tq, dh))
    qif = qi_ref[0].astype(F32)
    lane = lax.broadcasted_iota(I32, (tq, LANES), 1)
    per_tile = LANES // IDX_DIM
    for h in range(n_idx_heads):
        tile = qif[:, (h // per_tile) * LANES:(h // per_tile + 1) * LANES]
        shift = (h % per_tile) * IDX_DIM
        if shift:
            tile = pltpu.roll(tile, LANES - shift, 1)
        qie_ref[h] = jnp.where(lane < IDX_DIM, tile, 0.0).astype(BF16)
    wt = kiwq_ref[0].astype(F32).T[IDX_DIM:IDX_DIM + n_idx_heads] * IDX_SCALE

    nsb = (t0 + tq + tkb - 1) // tkb
    srow = lax.broadcasted_iota(I32, (tkb, tq), 0)
    tcol = t0 + lax.broadcasted_iota(I32, (tkb, tq), 1)

    def score_body(jb, carry):
        s0 = pl.multiple_of(jb * tkb, tkb)
        kib = kiw_ref[0, pl.ds(s0, tkb), :]
        sc = jnp.zeros((tkb, tq), F32)
        for h in range(n_idx_heads):
            z = lax.dot_general(kib, qie_ref[h], _NT, preferred_element_type=F32)
            sc += wt[h:h + 1, :] * jnp.maximum(z, 0.0)
        sc_ref[pl.ds(s0, tkb), :] = jnp.where(s0 + srow <= tcol, sc, -jnp.inf)
        return carry

    lax.fori_loop(0, nsb, score_body, 0)

    def count(pred):
        def body(jb, acc):
            s0 = pl.multiple_of(jb * tk, tk)
            ind = jnp.where(pred(sc_ref[pl.ds(s0, tk), :], s0 + srow_c), 1.0, 0.0)
            return acc + jnp.sum(ind.reshape(tk // COUNT_ROWS, COUNT_ROWS, tq), axis=0)
        acc = lax.fori_loop(0, nsc, body, jnp.zeros((COUNT_ROWS, tq), F32))
        return jnp.sum(acc, axis=0, keepdims=True)

    nsc = (t0 + tq + tk - 1) // tk
    srow_c = lax.broadcasted_iota(I32, (tk, tq), 0)

    kf = float(topk)
    c0 = count(lambda sc, sidx: sc >= 0.0)
    ok0 = c0 >= kf
    thr_key0 = jnp.where(ok0, jnp.int32(0), jnp.int32(INT_MIN))
    cnt0 = jnp.where(ok0, c0, 0.0)

    def thr_cond(carry):
        it, _, cnt = carry
        return jnp.logical_and(it < 31, jnp.max(jnp.abs(cnt - kf)) > 0.0)

    def bit_body(it, carry):
        thr_key, cnt = carry
        cand = thr_key | (jnp.int32(1) << (30 - it))
        cand_f = _key_to_float(cand)
        c = count(lambda sc, sidx: sc >= cand_f)
        ok = c >= kf
        return jnp.where(ok, cand, thr_key), jnp.where(ok, c, cnt)

    def thr_body(carry):
        it, thr_key, cnt = carry
        nxt = jnp.minimum(it + BITS_PER_CHECK, 31)
        thr_key, cnt = lax.fori_loop(it, nxt, bit_body, (thr_key, cnt))
        return nxt, thr_key, cnt

    _, thr_key, n_ge = lax.while_loop(thr_cond, thr_body, (jnp.int32(0), thr_key0, cnt0))
    thr = jnp.where(thr_key == INT_MIN, -jnp.inf, _key_to_float(thr_key))
    has_ties = jnp.max(n_ge) > kf
    tie_passes = jnp.where(has_ties, 1, 0)
    n_gt = lax.fori_loop(0, tie_passes, lambda it, c: count(lambda sc, sidx: sc > thr), jnp.zeros((1, tq), F32))
    need = kf - n_gt

    def cut_body(it, cut):
        cand = cut | (jnp.int32(1) << (12 - it))
        c = count(lambda sc, sidx: jnp.logical_and(sc == thr, sidx < cand))
        return jnp.where(c <= need, cand, cut)

    cut = lax.fori_loop(0, jnp.where(has_ties, 13, 0), cut_body, jnp.zeros((1, tq), I32))
    cut = jnp.where(has_ties, cut, jnp.int32(2 ** 30))

    nkb = (t0 + tq + tk - 1) // tk
    srow_k = lax.broadcasted_iota(I32, (tk, tq), 0)
    tcol_k = t0 + lax.broadcasted_iota(I32, (tk, tq), 1)

    def set_block_bias(jb):
        s0 = pl.multiple_of(jb * tk, tk)
        sc = sc_ref[pl.ds(s0, tk), :]
        sidx = s0 + srow_k
        sel = jnp.logical_or(sc > thr, jnp.logical_and(sc == thr, sidx < cut))
        sel = jnp.logical_and(sel, sidx <= tcol_k)
        bias_ref[...] = jnp.where(sel, 0.0, NEG_BIG)
        return s0

    def logits(h, kb):
        return lax.dot_general(kb, qs_ref[h], _NT, preferred_element_type=F32)

    m_ref[...] = jnp.full(m_ref.shape, NEG_BIG, F32)
    acc_ref[...] = jnp.zeros_like(acc_ref)

    def attn_body(jb, carry):
        s0 = set_block_bias(jb)
        maxima = []
        for g in range(kv_heads):
            kb = kext_ref[g, pl.ds(s0, tk), :]
            for r in range(rep):
                h = g * rep + r
                x = logits(h, kb) + bias_ref[...]
                xs_ref[h] = x
                maxima.append(jnp.max(x, axis=0, keepdims=True))
        for g in range(kv_heads):
            vt = vt_ref[g, :, pl.ds(s0, tk)]
            for r in range(rep):
                h = g * rep + r
                m_prev = m_ref[h]
                m_new = jnp.maximum(m_prev, maxima[h])
                p = jnp.exp2(xs_ref[h] - m_new).astype(BF16)
                acc_ref[h] = jnp.exp2(m_prev - m_new) * acc_ref[h] + jnp.dot(
                    vt, p, preferred_element_type=F32)
                m_ref[h] = m_new
        return carry

    lax.fori_loop(0, nkb, attn_body, 0)
    for p in range(n_heads // 2):
        outs = []
        for h in (2 * p, 2 * p + 1):
            acc = acc_ref[h]
            outs.append(acc[:dh] / acc[dh:dh + 1])
        o_ref[0, :, p * LANES:(p + 1) * LANES] = jnp.concatenate(outs, axis=0).T.astype(o_ref.dtype)


def _dsa_attention(proj, topk, tq=256, tkb=512, tk=256):
    b, s, n = proj.shape
    dh = HEAD_DIM
    qw = DSA_HEADS * dh
    kvw = DSA_KV_HEADS * dh
    iw = IDX_HEADS * IDX_DIM
    assert qw % kvw == 0 and (qw + 2 * kvw) % iw == 0 and (qw + 2 * kvw + iw) % LANES == 0
    assert IDX_DIM + IDX_HEADS <= LANES and n == qw + 2 * kvw + iw + LANES
    q_cols, k_cols = _alibi_columns(s)
    tkb = min(tkb, s)
    tk = min(tk, s)
    kiw_block = (qw + 2 * kvw + iw) // LANES
    dv = dh + BF16_SUBLANES
    return pl.pallas_call(
        functools.partial(_dsa_kernel, tq=tq, tkb=tkb, tk=tk, topk=topk),
        grid=(b, s // tq),
        in_specs=[pl.BlockSpec((1, tq, qw), lambda bi, i: (bi, i, 0)),
                  pl.BlockSpec((1, s, kvw), lambda bi, i: (bi, 0, qw // kvw)),
                  pl.BlockSpec((1, s, kvw), lambda bi, i: (bi, 0, qw // kvw + 1)),
                  pl.BlockSpec((1, tq, iw), lambda bi, i: (bi, i, (qw + 2 * kvw) // iw)),
                  pl.BlockSpec((1, s, LANES), lambda bi, i: (bi, 0, kiw_block)),
                  pl.BlockSpec((1, tq, LANES), lambda bi, i: (bi, i, kiw_block)),
                  pl.BlockSpec((DSA_HEADS, 1, dh), lambda bi, i: (0, 0, 0)),
                  pl.BlockSpec((s, dh), lambda bi, i: (0, 0))],
        out_specs=pl.BlockSpec((1, tq, qw), lambda bi, i: (bi, i, 0)),
        out_shape=jax.ShapeDtypeStruct((b, s, qw), BF16),
        scratch_shapes=[pltpu.VMEM((DSA_KV_HEADS, s, 2 * dh), BF16),
                        pltpu.VMEM((DSA_KV_HEADS, dv, s), BF16),
                        pltpu.VMEM((IDX_HEADS, tq, LANES), BF16),
                        pltpu.VMEM((s, tq), F32),
                        pltpu.VMEM((DSA_HEADS, 1, tq), F32),
                        pltpu.VMEM((DSA_HEADS, dv, tq), F32),
                        pltpu.VMEM((DSA_HEADS, tq, 2 * dh), BF16),
                        pltpu.VMEM((tk, tq), F32),
                        pltpu.VMEM((DSA_HEADS, tk, tq), F32)],
        compiler_params=_params("parallel", "arbitrary"),
        name="dsa_attention",
    )(proj, proj, proj, proj, proj, proj, q_cols, k_cols)


def _row(v):
    return v.astype(F32)[None]


def _even_layer(x2d, bsz, s, w_in, ssm_ops, w_glu, b_glu, w_out, g_mix, b_mix, mlp):
    proj = _proj(x2d, w_in.astype(BF16)).reshape(bsz, s, -1)
    a_out = _sb_attention(proj).reshape(bsz * s, SB_WIDTH)
    y = _ssm_scan(proj, *ssm_ops, nch=s // (SSM_CHUNK * SUBLANES), nseg=SUBLANES).reshape(bsz * s, SSM_WIDTH)
    consts = (w_glu.astype(BF16), _row(b_glu), w_out.astype(BF16), _row(g_mix), _row(b_mix)) + mlp
    return _layer_tail(_tail_even_kernel, "tail_even", (a_out, y, x2d), consts)


def _odd_layer(x2d, bsz, s, w_in, w_out, g_mix, b_mix, mlp):
    proj = _proj(x2d, w_in.astype(BF16)).reshape(bsz, s, -1)
    o = _dsa_attention(proj, min(TOPK_MAX, s // 4)).reshape(bsz * s, DSA_HEADS * HEAD_DIM)
    consts = (w_out.astype(BF16), _row(g_mix), _row(b_mix)) + mlp
    return _layer_tail(_tail_odd_kernel, "tail_odd", (o, x2d), consts)


def kernel(x, sb_ssm_w_in, ssm_log_dt, ssm_lam_re, ssm_lam_im, ssm_b_re, ssm_b_im, ssm_c_re, ssm_c_im, ssm_d,
           ssm_w_glu, ssm_b_glu, sb_ssm_w_out, dsa_w_in, dsa_w_out, ln_mix_g, ln_mix_b, ln_ffn_g, ln_ffn_b,
           mlp_w1, mlp_w2):
    bsz, s, d = x.shape
    x2d = x.reshape(bsz * s, d).astype(F32)
    depth = mlp_w1.shape[0]
    stacked = [p.reshape((-1,) + p.shape[2:]) for p in
               (ssm_log_dt, ssm_lam_re, ssm_lam_im, ssm_b_re, ssm_b_im, ssm_c_re, ssm_c_im, ssm_d)]
    ssm_ops = _ssm_operators(*stacked, nch=s // (SSM_CHUNK * SUBLANES))
    tiles = SSM_WIDTH // LANES
    for i in range(depth):
        j = i // 2
        mlp = (mlp_w1[i].astype(BF16), mlp_w2[i].astype(BF16), _row(ln_ffn_g[i]), _row(ln_ffn_b[i]))
        if i % 2 == 0:
            layer_ops = tuple(op[j * tiles:(j + 1) * tiles] for op in ssm_ops)
            x2d = _even_layer(x2d, bsz, s, sb_ssm_w_in[j], layer_ops, ssm_w_glu[j], ssm_b_glu[j],
                              sb_ssm_w_out[j], ln_mix_g[i], ln_mix_b[i], mlp)
        else:
            n_in = dsa_w_in.shape[2]
            pad = (-n_in) % LANES
            w_in = jnp.pad(dsa_w_in[j], ((0, 0), (0, pad)))
            x2d = _odd_layer(x2d, bsz, s, w_in, dsa_w_out[j], ln_mix_g[i], ln_mix_b[i], mlp)
    return x2d.reshape(bsz, s, d).astype(x.dtype)
```
